```python
import jax, jax.numpy as jnp
from jax import lax
import numpy as np

D_MODEL = 2048
BATCH = 16
SEQ = 2048
DEPTH = 4

D_MIX = D_MODEL
CONV_CH = D_MIX // 2
CONV_GROUPS = 8
CONV_WIDTH = 31
DN_HEADS = 8
DN_HEAD_DIM = (D_MIX - CONV_CH) // DN_HEADS
DN_DIM = DN_HEADS * DN_HEAD_DIM
SHORT_CONV = 4
CHUNK = 64
D_FF = 4 * D_MODEL
N_MOD = 6
EPS = 1e-6
IN_COLS = 2 * CONV_CH + 4 * DN_DIM + 2 * DN_HEADS

kernel_name = "hymba_style_conformer_gdn_hybrid"


def rmsnorm(x, w):
    xf = x.astype(jnp.float32)
    y = xf * lax.rsqrt(jnp.mean(xf * xf, axis=-1, keepdims=True) + EPS)
    return (y * w.astype(jnp.float32)).astype(x.dtype)


def layernorm(x, w, b):
    xf = x.astype(jnp.float32)
    mu = jnp.mean(xf, axis=-1, keepdims=True)
    xc = xf - mu
    y = xc * lax.rsqrt(jnp.mean(xc * xc, axis=-1, keepdims=True) + EPS)
    return (y * w.astype(jnp.float32) + b.astype(jnp.float32)).astype(x.dtype)


def l2norm(x):
    return x * lax.rsqrt(jnp.sum(x * x, axis=-1, keepdims=True) + EPS)


def causal_depthwise_conv(x, w):
    K, C = w.shape
    return lax.conv_general_dilated(
        x, w[:, None, :].astype(x.dtype), window_strides=(1,), padding=[(K - 1, 0)],
        dimension_numbers=("NWC", "WIO", "NWC"), feature_group_count=C)


def gated_delta_rule_chunked(q, k, v, g, beta):
    B, S, H, Dk = q.shape
    Dv = v.shape[-1]
    NC = S // CHUNK

    def to_chunks(t):
        return t.reshape(B, NC, CHUNK, H, t.shape[-1]).transpose(0, 3, 1, 2, 4)

    q = to_chunks(q) * (Dk ** -0.5)
    k = to_chunks(k)
    v = to_chunks(v)
    beta = beta.reshape(B, NC, CHUNK, H).transpose(0, 3, 1, 2)
    g = jnp.cumsum(g.reshape(B, NC, CHUNK, H).transpose(0, 3, 1, 2), axis=-1)

    k_beta = k * beta[..., None]
    v_beta = v * beta[..., None]
    causal = jnp.tril(jnp.ones((CHUNK, CHUNK), dtype=bool))
    strict = jnp.tril(jnp.ones((CHUNK, CHUNK), dtype=bool), -1)
    decay = jnp.exp(jnp.where(causal, g[..., :, None] - g[..., None, :], -jnp.inf))

    L = jnp.where(strict, jnp.einsum("bhncd,bhnmd->bhncm", k_beta, k) * decay, 0.0)
    A = L + jnp.eye(CHUNK, dtype=jnp.float32)
    rhs = jnp.concatenate([v_beta, k_beta * jnp.exp(g)[..., None]], axis=-1)
    sol = lax.linalg.triangular_solve(A, rhs, left_side=True, lower=True, unit_diagonal=True)
    u = sol[..., :Dv]
    w = sol[..., Dv:]

    attn = jnp.where(causal, jnp.einsum("bhncd,bhnmd->bhncm", q, k) * decay, 0.0)
    q_dec = q * jnp.exp(g)[..., None]
    k_dec = k * jnp.exp(g[..., -1:] - g)[..., None]
    g_last = jnp.exp(g[..., -1])

    xs = tuple(jnp.moveaxis(t, 2, 0) for t in (q_dec, k_dec, u, w, attn, g_last))

    def step(state, inp):
        qd, kd, u_c, w_c, a_c, gl = inp
        v_new = u_c - jnp.einsum("bhck,bhkv->bhcv", w_c, state)
        o = (jnp.einsum("bhck,bhkv->bhcv", qd, state)
             + jnp.einsum("bhcm,bhmv->bhcv", a_c, v_new))
        state = state * gl[..., None, None] + jnp.einsum("bhck,bhcv->bhkv", kd, v_new)
        return state, o

    s0 = jnp.zeros((B, H, Dk, Dv), jnp.float32)
    _, o = lax.scan(step, s0, xs)
    return o.transpose(1, 0, 3, 2, 4).reshape(B, S, H, Dv)


def modulate(h, shift, scale):
    return h * (1.0 + scale[:, None, :]) + shift[:, None, :]


def hybrid_layer(x, mod, norm1_w, w_in, conv_dw_w, conv_dw_b, conv_ln_w, conv_ln_b, w_pw2,
                 conv_out_norm_w, qkv_conv_w, a_log, dt_bias, dn_norm_w, w_out,
                 norm2_w, w_up, w_down):
    B, S, _ = x.shape
    shift1, scale1, gate1, shift2, scale2, gate2 = jnp.split(mod, N_MOD, axis=-1)

    h = modulate(rmsnorm(x, norm1_w), shift1, scale1)
    proj = h @ w_in
    cuts = np.cumsum([CONV_CH, CONV_CH, DN_DIM, DN_DIM, DN_DIM, DN_DIM, DN_HEADS])
    c_val, c_gate, q, k, v, z, b_logit, a_logit = jnp.split(proj, cuts, axis=-1)

    u = c_val * jax.nn.sigmoid(c_gate)
    u = causal_depthwise_conv(u, conv_dw_w) + conv_dw_b
    u = jax.nn.silu(layernorm(u, conv_ln_w, conv_ln_b))
    u = u @ w_pw2
    y_conv = rmsnorm(u, conv_out_norm_w)

    qkv = jax.nn.silu(causal_depthwise_conv(jnp.concatenate([q, k, v], axis=-1), qkv_conv_w))
    q, k, v = jnp.split(qkv.astype(jnp.float32), 3, axis=-1)
    q = l2norm(q.reshape(B, S, DN_HEADS, DN_HEAD_DIM))
    k = l2norm(k.reshape(B, S, DN_HEADS, DN_HEAD_DIM))
    v = v.reshape(B, S, DN_HEADS, DN_HEAD_DIM)
    beta = jax.nn.sigmoid(b_logit.astype(jnp.float32))
    g = -jnp.exp(a_log.astype(jnp.float32)) * jax.nn.softplus(
        a_logit.astype(jnp.float32) + dt_bias.astype(jnp.float32))
    o = gated_delta_rule_chunked(q, k, v, g, beta)
    o = rmsnorm(o, dn_norm_w) * jax.nn.silu(
        z.astype(jnp.float32).reshape(B, S, DN_HEADS, DN_HEAD_DIM))
    y_dn = o.reshape(B, S, DN_DIM).astype(x.dtype)

    y = jnp.concatenate([y_conv, y_dn], axis=-1) @ w_out
    x = x + gate1[:, None, :] * y

    h2 = modulate(rmsnorm(x, norm2_w), shift2, scale2)
    m = jnp.square(jax.nn.relu(h2 @ w_up)) @ w_down
    return x + gate2[:, None, :] * m


def setup_inputs(seed: int = 0) -> dict:
    key = jax.random.key(seed)
    ks = jax.random.split(key, 24)
    f32 = jnp.float32
    nrm = lambda k, shape, s: jax.random.normal(k, shape, f32) * s
    dt = jnp.exp(jax.random.uniform(ks[15], (DEPTH, DN_HEADS), f32, np.log(1e-3), np.log(1e-1)))
    return {
        "x": nrm(ks[0], (BATCH, SEQ, D_MODEL), 1.0),
        "c": nrm(ks[1], (BATCH, D_MODEL), 1.0),
        "w_ada": nrm(ks[2], (DEPTH, D_MODEL, N_MOD * D_MODEL), 0.5 * D_MODEL ** -0.5),
        "b_ada": nrm(ks[3], (DEPTH, N_MOD * D_MODEL), 0.01),
        "norm1_w": 1.0 + nrm(ks[4], (DEPTH, D_MODEL), 0.02),
        "w_in": nrm(ks[5], (DEPTH, D_MODEL, IN_COLS), D_MODEL ** -0.5),
        "conv_dw_w": nrm(ks[6], (DEPTH, CONV_WIDTH, CONV_CH), CONV_WIDTH ** -0.5),
        "conv_dw_b": nrm(ks[7], (DEPTH, CONV_CH), 0.02),
        "conv_ln_w": 1.0 + nrm(ks[8], (DEPTH, CONV_CH), 0.02),
        "conv_ln_b": nrm(ks[9], (DEPTH, CONV_CH), 0.02),
        "w_pw2": nrm(ks[10], (DEPTH, CONV_CH, CONV_CH), CONV_CH ** -0.5),
        "conv_out_norm_w": 1.0 + nrm(ks[11], (DEPTH, CONV_CH), 0.02),
        "qkv_conv_w": nrm(ks[12], (DEPTH, SHORT_CONV, 3 * DN_DIM), SHORT_CONV ** -0.5),
        "a_log": jnp.log(jax.random.uniform(ks[13], (DEPTH, DN_HEADS), f32, 1.0, 16.0)),
        "dt_bias": dt + jnp.log(-jnp.expm1(-dt)),
        "dn_norm_w": 1.0 + nrm(ks[14], (DEPTH, DN_HEAD_DIM), 0.02),
        "w_out": nrm(ks[16], (DEPTH, D_MIX, D_MODEL), D_MIX ** -0.5),
        "norm2_w": 1.0 + nrm(ks[17], (DEPTH, D_MODEL), 0.02),
        "w_up": nrm(ks[18], (DEPTH, D_MODEL, D_FF), D_MODEL ** -0.5),
        "w_down": nrm(ks[19], (DEPTH, D_FF, D_MODEL), D_FF ** -0.5),
        "final_ada_w": nrm(ks[20], (D_MODEL, 2 * D_MODEL), 0.5 * D_MODEL ** -0.5),
        "final_ada_b": nrm(ks[21], (2 * D_MODEL,), 0.01),
        "final_norm_w": 1.0 + nrm(ks[22], (D_MODEL,), 0.02),
    }


def reference(x, c, w_ada, b_ada, norm1_w, w_in, conv_dw_w, conv_dw_b, conv_ln_w, conv_ln_b,
              w_pw2, conv_out_norm_w, qkv_conv_w, a_log, dt_bias, dn_norm_w, w_out,
              norm2_w, w_up, w_down, final_ada_w, final_ada_b, final_norm_w):
    c_act = jax.nn.silu(c)
    for l in range(DEPTH):
        mod = c_act @ w_ada[l] + b_ada[l]
        x = hybrid_layer(x, mod, norm1_w[l], w_in[l], conv_dw_w[l], conv_dw_b[l],
                         conv_ln_w[l], conv_ln_b[l], w_pw2[l], conv_out_norm_w[l],
                         qkv_conv_w[l], a_log[l], dt_bias[l], dn_norm_w[l], w_out[l],
                         norm2_w[l], w_up[l], w_down[l])
    shift_f, scale_f = jnp.split(c_act @ final_ada_w + final_ada_b, 2, axis=-1)
    return modulate(rmsnorm(x, final_norm_w), shift_f, scale_f)
```

```python
import functools

import jax
import jax.numpy as jnp
from jax import lax
from jax.experimental import pallas as pl
from jax.experimental.pallas import tpu as pltpu

EPS = 1e-6
CHUNK = 64
SUBLANES = 8
LANES = 128
VMEM_LIMIT = 56 * 1024 * 1024
BF = jnp.bfloat16
F32 = jnp.float32
HI = lax.Precision.HIGHEST


def _round_up(n, m):
    return (n + m - 1) // m * m


def _pick(n, pref):
    t = min(n, pref)
    while n % t:
        t //= 2
    return t


def _params(sem):
    return pltpu.CompilerParams(dimension_semantics=sem, vmem_limit_bytes=VMEM_LIMIT)


def _sigmoid(v):
    return jax.nn.sigmoid(v)


def _silu(v):
    return v * jax.nn.sigmoid(v)


def _ada_kernel(c_ref, w_ref, b_ref, o_ref):
    ca = _silu(c_ref[...]).astype(BF)
    o_ref[...] = jnp.dot(ca, w_ref[...].astype(BF), preferred_element_type=F32) + b_ref[...]


def _ada(c, w, b):
    nl, d, n = w.shape
    nb = c.shape[0]
    tn = _pick(n, 1024)
    return pl.pallas_call(
        _ada_kernel,
        grid=(nl, n // tn),
        in_specs=[
            pl.BlockSpec((nb, d), lambda l, j: (0, 0)),
            pl.BlockSpec((None, d, tn), lambda l, j: (l, 0, j)),
            pl.BlockSpec((None, 1, tn), lambda l, j: (l, 0, j)),
        ],
        out_specs=pl.BlockSpec((None, nb, tn), lambda l, j: (l, 0, j)),
        out_shape=jax.ShapeDtypeStruct((nl, nb, n), F32),
        compiler_params=_params(("parallel", "parallel")),
        name="ada",
    )(c, w, b.reshape(nl, 1, n))


def _norm_mod_rows(x_ref, nw_ref, shift_ref, scale_ref, h_ref, rows_per_step):
    tm = x_ref.shape[0]
    nw = nw_ref[...]
    mul = 1.0 + scale_ref[...]
    shift = shift_ref[...]

    def body(r, carry):
        r0 = pl.multiple_of(r * rows_per_step, rows_per_step)
        xr = x_ref[pl.ds(r0, rows_per_step), :]
        inv = lax.rsqrt(jnp.mean(xr * xr, axis=-1, keepdims=True) + EPS)
        h_ref[pl.ds(r0, rows_per_step), :] = ((xr * inv * nw) * mul + shift).astype(h_ref.dtype)
        return carry

    lax.fori_loop(0, tm // rows_per_step, body, 0)


def _inproj_kernel(x_ref, shift_ref, scale_ref, nw_ref, w_ref, wba_ref, proj_ref, ba_ref, h_ref):
    @pl.when(pl.program_id(1) == 0)
    def _():
        _norm_mod_rows(x_ref, nw_ref, shift_ref, scale_ref, h_ref, 16)
        ba_ref[...] = lax.dot_general(wba_ref[...], h_ref[...], (((1,), (1,)), ((), ())),
                                      preferred_element_type=F32)

    proj_ref[...] = jnp.dot(h_ref[...], w_ref[...], preferred_element_type=F32)


def _inproj(x2, mod4, nw, w_main, w_ba_t, seq):
    t, d = x2.shape
    n = w_main.shape[1]
    nba = w_ba_t.shape[0]
    tm = _pick(seq, 1024)
    tn = _pick(n, 1024)
    per_b = seq // tm
    return pl.pallas_call(
        _inproj_kernel,
        grid=(t // tm, n // tn),
        in_specs=[
            pl.BlockSpec((tm, d), lambda i, j: (i, 0)),
            pl.BlockSpec((None, 1, d), lambda i, j: (i // per_b, 0, 0)),
            pl.BlockSpec((None, 1, d), lambda i, j: (i // per_b, 0, 1)),
            pl.BlockSpec((1, d), lambda i, j: (0, 0)),
            pl.BlockSpec((d, tn), lambda i, j: (0, j)),
            pl.BlockSpec((nba, d), lambda i, j: (0, 0)),
        ],
        out_specs=[
            pl.BlockSpec((tm, tn), lambda i, j: (i, j)),
            pl.BlockSpec((nba, tm), lambda i, j: (0, i)),
        ],
        out_shape=[
            jax.ShapeDtypeStruct((t, n), F32),
            jax.ShapeDtypeStruct((nba, t), F32),
        ],
        scratch_shapes=[pltpu.VMEM((tm, d), BF)],
        compiler_params=_params(("parallel", "arbitrary")),
        name="inproj",
    )(x2, mod4, mod4, nw, w_main, w_ba_t)


def _conv_kernel(val_ref, gate_ref, dww_ref, dwb_ref, lnw_ref, lnb_ref, pw_ref, onw_ref,
                 y_ref, ubuf_ref, cbuf_ref, abuf_ref, *, kw, halo, rb, cr, cw):
    ts, cc = val_ref.shape

    @pl.when(pl.program_id(1) == 0)
    def _():
        ubuf_ref[pl.ds(0, halo), :] = jnp.zeros((halo, cc), F32)

    def glu(r, carry):
        r0 = pl.multiple_of(r * rb, rb)
        ubuf_ref[pl.ds(halo + r0, rb), :] = (
            val_ref[pl.ds(r0, rb), :] * _sigmoid(gate_ref[pl.ds(r0, rb), :]))
        return carry

    lax.fori_loop(0, ts // rb, glu, 0)

    lnw = lnw_ref[...]
    lnb = lnb_ref[...]

    first = halo - (kw - 1)
    for cb in range(cc // cw):
        cols = slice(cb * cw, (cb + 1) * cw)
        bias = dwb_ref[:, cols]

        def conv(r, carry, cols=cols, bias=bias):
            r0 = pl.multiple_of(r * cr, cr)
            win = ubuf_ref[pl.ds(r0, cr + halo), cols]
            acc = jnp.broadcast_to(bias, (cr, cw))
            for p in range(SUBLANES):
                offs = [o for o in range(first, halo + 1) if o % SUBLANES == p]
                if not offs:
                    continue
                wp = win if p == 0 else win[p:p + cr + halo - SUBLANES]
                for o in offs:
                    a = o - p
                    acc = acc + wp[a:a + cr] * dww_ref[pl.ds(o - first, 1), cols]
            cbuf_ref[pl.ds(r0, cr), cols] = acc
            return carry

        lax.fori_loop(0, ts // cr, conv, 0)

    def lnorm(r, carry):
        r0 = pl.multiple_of(r * rb, rb)
        acc = cbuf_ref[pl.ds(r0, rb), :]
        mu = jnp.mean(acc, axis=-1, keepdims=True)
        xc = acc - mu
        var = jnp.mean(xc * xc, axis=-1, keepdims=True)
        yn = xc * lax.rsqrt(var + EPS) * lnw + lnb
        abuf_ref[pl.ds(r0, rb), :] = _silu(yn).astype(BF)
        return carry

    lax.fori_loop(0, ts // rb, lnorm, 0)

    ubuf_ref[pl.ds(0, halo), :] = ubuf_ref[pl.ds(ts, halo), :]

    p = jnp.dot(abuf_ref[...], pw_ref[...], preferred_element_type=F32)
    inv = lax.rsqrt(jnp.mean(p * p, axis=-1, keepdims=True) + EPS)
    y_ref[...] = (p * inv * onw_ref[...]).astype(y_ref.dtype)


def _conv_group(proj, dww, dwb, lnw, lnb, pw, onw, batch, seq):
    kw, cc = dww.shape
    halo = _round_up(kw - 1, SUBLANES)
    ts = _pick(seq, 256)
    ns = seq // ts
    t = batch * seq
    row = lambda a: a.reshape(1, cc)
    return pl.pallas_call(
        functools.partial(_conv_kernel, kw=kw, halo=halo, rb=16, cr=_pick(ts, 64), cw=_pick(cc, 256)),
        grid=(batch, ns),
        in_specs=[
            pl.BlockSpec((ts, cc), lambda b, s: (b * ns + s, 0)),
            pl.BlockSpec((ts, cc), lambda b, s: (b * ns + s, 1)),
            pl.BlockSpec((kw, cc), lambda b, s: (0, 0)),
            pl.BlockSpec((1, cc), lambda b, s: (0, 0)),
            pl.BlockSpec((1, cc), lambda b, s: (0, 0)),
            pl.BlockSpec((1, cc), lambda b, s: (0, 0)),
            pl.BlockSpec((cc, cc), lambda b, s: (0, 0)),
            pl.BlockSpec((1, cc), lambda b, s: (0, 0)),
        ],
        out_specs=pl.BlockSpec((ts, cc), lambda b, s: (b * ns + s, 0)),
        out_shape=jax.ShapeDtypeStruct((t, cc), BF),
        scratch_shapes=[
            pltpu.VMEM((halo + ts, cc), F32),
            pltpu.VMEM((ts, cc), F32),
            pltpu.VMEM((ts, cc), BF),
        ],
        compiler_params=_params(("arbitrary", "arbitrary")),
        name="conv_group",
    )(proj, proj, dww, row(dwb), row(lnw), row(lnb), pw, row(onw))


def _dot_nt(a, b, precision=None):
    return lax.dot_general(a, b, (((1,), (1,)), ((), ())), precision=precision,
                           preferred_element_type=F32)


def _dot32(a, b):
    return jnp.dot(a, b, precision=HI, preferred_element_type=F32)


def _unit_lower_inverse(lmat):
    n = lmat.shape[0]
    ri = lax.broadcasted_iota(jnp.int32, (n, n), 0)
    ci = lax.broadcasted_iota(jnp.int32, (n, n), 1)
    eye = (ri == ci).astype(F32)
    same16 = (ri // 16) == (ci // 16)
    same32 = (ri // 32) == (ci // 32)
    l16 = jnp.where(same16, lmat, 0.0)
    l2 = _dot32(l16, l16)
    l4 = _dot32(l2, l2)
    l8 = _dot32(l4, l4)
    x = eye - l16
    x = x + _dot32(x, l2)
    x = x + _dot32(x, l4)
    x = x + _dot32(x, l8)
    c1 = jnp.where(same32 & jnp.logical_not(same16), lmat, 0.0)
    x = x - _dot32(_dot32(x, c1), x)
    c2 = jnp.where(jnp.logical_not(same32), lmat, 0.0)
    x = x - _dot32(_dot32(x, c2), x)
    return x


def _delta_kernel(alog_ref, dtb_ref, q_ref, k_ref, v_ref, z_ref, ba_ref, wq_ref, wk_ref, wv_ref,
                  nw_ref, y_ref, state_ref, halo_ref, ext_ref, *, heads, sc):
    ts, dh = q_ref.shape
    h = pl.program_id(1)
    hrows = halo_ref.shape[1]

    @pl.when(pl.program_id(2) == 0)
    def _():
        state_ref[...] = jnp.zeros_like(state_ref)
        halo_ref[...] = jnp.zeros_like(halo_ref)

    def short_conv(idx, x_ref, w_ref):
        ext_ref[pl.ds(0, hrows), :] = halo_ref[idx]
        ext_ref[pl.ds(hrows, ts), :] = x_ref[...]
        acc = jnp.zeros((ts, dh), F32)
        for j in range(sc):
            off = hrows - (sc - 1) + j
            acc = acc + ext_ref[pl.ds(off, ts), :] * w_ref[pl.ds(j, 1), :]
        halo_ref[idx] = x_ref[pl.ds(ts - hrows, hrows), :]
        return _silu(acc)

    q = short_conv(0, q_ref, wq_ref)
    k = short_conv(1, k_ref, wk_ref)
    v = short_conv(2, v_ref, wv_ref)
    q = q * lax.rsqrt(jnp.sum(q * q, axis=-1, keepdims=True) + EPS) * (dh ** -0.5)
    k = k * lax.rsqrt(jnp.sum(k * k, axis=-1, keepdims=True) + EPS)

    b_logit = ba_ref[pl.ds(h, 1), :]
    a_logit = ba_ref[pl.ds(heads + h, 1), :]
    beta_row = _sigmoid(b_logit)
    a_neg = -jnp.exp(jnp.full((1, ts), alog_ref[h], F32))
    xs = a_logit + dtb_ref[h]
    softplus = jnp.maximum(xs, 0.0) + jnp.log(1.0 + jnp.exp(-jnp.abs(xs)))
    g_rows = jnp.broadcast_to(a_neg * softplus, (SUBLANES, ts))
    lane = lax.broadcasted_iota(jnp.int32, (SUBLANES, ts), 1) % CHUNK
    d = 1
    while d < CHUNK:
        g_rows = g_rows + jnp.where(lane >= d, pltpu.roll(g_rows, d, 1), 0.0)
        d *= 2
    g_row = g_rows[0:1]
    sub = lax.broadcasted_iota(jnp.int32, (SUBLANES, ts), 0)
    rows = jnp.where(sub == 0, jnp.broadcast_to(beta_row, (SUBLANES, ts)), g_rows)
    rows = jnp.concatenate([rows, jnp.zeros((LANES - SUBLANES, ts), F32)], axis=0)
    cols = rows.T
    beta_col = cols[:, 0:1]
    g_col = cols[:, 1:2]

    ri = lax.broadcasted_iota(jnp.int32, (CHUNK, CHUNK), 0)
    ci = lax.broadcasted_iota(jnp.int32, (CHUNK, CHUNK), 1)
    causal = ri >= ci
    strict = ri > ci
    di = lax.broadcasted_iota(jnp.int32, (dh, dh), 0)
    dj = lax.broadcasted_iota(jnp.int32, (dh, dh), 1)
    eye_dh = (di == dj).astype(BF)
    nw = nw_ref[...]

    for c in range(ts // CHUNK):
        r0 = c * CHUNK
        qc = q[r0:r0 + CHUNK]
        kc = k[r0:r0 + CHUNK]
        vc = v[r0:r0 + CHUNK]
        bc = beta_col[r0:r0 + CHUNK]
        gc = g_col[r0:r0 + CHUNK]
        gr = g_row[:, r0:r0 + CHUNK]
        g_last = gc[CHUNK - 1:CHUNK, :]
        decay = jnp.exp(jnp.where(causal, gc - gr, -jnp.inf))
        kb = kc * bc
        kc16 = kc.astype(BF)
        lmat = jnp.where(strict, _dot_nt(kb.astype(BF), kc16) * decay, 0.0)
        attn = jnp.where(causal, _dot_nt(qc.astype(BF), kc16) * decay, 0.0)
        tinv = _unit_lower_inverse(lmat)
        eg = jnp.exp(gc)
        rhs = jnp.concatenate([vc * bc, kb * eg], axis=1)
        sol = _dot32(tinv, rhs)
        u = sol[:, :dh]
        w = sol[:, dh:]
        state = state_ref[...]
        s16 = state.astype(BF)
        v_new = u - jnp.dot(w.astype(BF), s16, preferred_element_type=F32)
        vn16 = v_new.astype(BF)
        o = (jnp.dot((qc * eg).astype(BF), s16, preferred_element_type=F32)
             + jnp.dot(attn.astype(BF), vn16, preferred_element_type=F32))
        kd16 = (kc * jnp.exp(g_last - gc)).astype(BF)
        kd_t = _dot_nt(eye_dh, kd16).astype(BF)
        state_ref[...] = state * jnp.exp(g_last) + jnp.dot(kd_t, vn16, preferred_element_type=F32)
        zc = z_ref[pl.ds(r0, CHUNK), :]
        on = o * lax.rsqrt(jnp.mean(o * o, axis=-1, keepdims=True) + EPS) * nw
        y_ref[pl.ds(r0, CHUNK), :] = (on * _silu(zc)).astype(y_ref.dtype)


def _delta_group(proj, ba_t, conv_w, a_log, dt_bias, nw, batch, seq, cc, heads, dh):
    sc = conv_w.shape[0]
    dn = heads * dh
    ts = _pick(seq, 256)
    ns = seq // ts
    t = batch * seq
    nba = ba_t.shape[0]
    cb = 2 * cc // dh
    hrows = _round_up(sc - 1, SUBLANES)
    tok = lambda off: pl.BlockSpec((ts, dh), lambda b, h, s: (b * ns + s, off + h))
    cw = lambda off: pl.BlockSpec((sc, dh), lambda b, h, s: (0, off + h))
    smem = pl.BlockSpec(memory_space=pltpu.SMEM)
    return pl.pallas_call(
        functools.partial(_delta_kernel, heads=heads, sc=sc),
        grid=(batch, heads, ns),
        in_specs=[
            smem, smem,
            tok(cb), tok(cb + heads), tok(cb + 2 * heads), tok(cb + 3 * heads),
            pl.BlockSpec((nba, ts), lambda b, h, s: (0, b * ns + s)),
            cw(0), cw(heads), cw(2 * heads),
            pl.BlockSpec((1, dh), lambda b, h, s: (0, 0)),
        ],
        out_specs=pl.BlockSpec((ts, dh), lambda b, h, s: (b * ns + s, h)),
        out_shape=jax.ShapeDtypeStruct((t, dn), BF),
        scratch_shapes=[
            pltpu.VMEM((dh, dh), F32),
            pltpu.VMEM((3, hrows, dh), F32),
            pltpu.VMEM((hrows + ts, dh), F32),
        ],
        compiler_params=_params(("arbitrary", "arbitrary", "arbitrary")),
        name="delta_group",
    )(a_log, dt_bias, proj, proj, proj, proj, ba_t, conv_w, conv_w, conv_w, nw.reshape(1, dh))


def _outproj_kernel(x_ref, yc_ref, yd_ref, wc_ref, wd_ref, gate_ref, o_ref):
    y = (jnp.dot(yc_ref[...], wc_ref[...], preferred_element_type=F32)
         + jnp.dot(yd_ref[...], wd_ref[...], preferred_element_type=F32))
    o_ref[...] = x_ref[...] + gate_ref[...] * y


def _outproj(x2, y_conv, y_dn, w_out, mod4, seq):
    t, d = x2.shape
    cc = y_conv.shape[1]
    dn = y_dn.shape[1]
    tm = _pick(seq, 512)
    per_b = seq // tm
    return pl.pallas_call(
        _outproj_kernel,
        grid=(t // tm,),
        in_specs=[
            pl.BlockSpec((tm, d), lambda i: (i, 0)),
            pl.BlockSpec((tm, cc), lambda i: (i, 0)),
            pl.BlockSpec((tm, dn), lambda i: (i, 0)),
            pl.BlockSpec((cc, d), lambda i: (0, 0)),
            pl.BlockSpec((dn, d), lambda i: (0, 0)),
            pl.BlockSpec((None, 1, d), lambda i: (i // per_b, 0, 2)),
        ],
        out_specs=pl.BlockSpec((tm, d), lambda i: (i, 0)),
        out_shape=jax.ShapeDtypeStruct((t, d), F32),
        compiler_params=_params(("parallel",)),
        name="outproj",
    )(x2, y_conv, y_dn, w_out[:cc], w_out[cc:], mod4)


def _mlp_kernel(x_ref, shift_ref, scale_ref, gate_ref, nw_ref, wu_ref, wd_ref, o_ref, h_ref):
    j = pl.program_id(1)

    @pl.when(j == 0)
    def _():
        _norm_mod_rows(x_ref, nw_ref, shift_ref, scale_ref, h_ref, 16)

    a = jnp.maximum(jnp.dot(h_ref[...], wu_ref[...], preferred_element_type=F32), 0.0)
    m = jnp.dot((a * a).astype(BF), wd_ref[...], preferred_element_type=F32)

    @pl.when(j == 0)
    def _():
        o_ref[...] = m

    @pl.when(j > 0)
    def _():
        o_ref[...] += m

    @pl.when(j == pl.num_programs(1) - 1)
    def _():
        o_ref[...] = x_ref[...] + gate_ref[...] * o_ref[...]


def _mlp(x2, mod4, nw, w_up, w_down, seq):
    t, d = x2.shape
    f = w_up.shape[1]
    tm = _pick(seq, 1024)
    tf = _pick(f, 512)
    per_b = seq // tm
    modspec = lambda which: pl.BlockSpec((None, 1, d), lambda i, j: (i // per_b, 0, which))
    return pl.pallas_call(
        _mlp_kernel,
        grid=(t // tm, f // tf),
        in_specs=[
            pl.BlockSpec((tm, d), lambda i, j: (i, 0)),
            modspec(3), modspec(4), modspec(5),
            pl.BlockSpec((1, d), lambda i, j: (0, 0)),
            pl.BlockSpec((d, tf), lambda i, j: (0, j)),
            pl.BlockSpec((tf, d), lambda i, j: (j, 0)),
        ],
        out_specs=pl.BlockSpec((tm, d), lambda i, j: (i, 0)),
        out_shape=jax.ShapeDtypeStruct((t, d), F32),
        scratch_shapes=[pltpu.VMEM((tm, d), BF)],
        compiler_params=_params(("parallel", "arbitrary")),
        name="mlp",
    )(x2, mod4, mod4, mod4, nw, w_up, w_down)


def _final_kernel(x_ref, shift_ref, scale_ref, nw_ref, o_ref):
    _norm_mod_rows(x_ref, nw_ref, shift_ref, scale_ref, o_ref, 16)


def _final(x2, modf, nw, seq):
    t, d = x2.shape
    tm = _pick(seq, 512)
    per_b = seq // tm
    return pl.pallas_call(
        _final_kernel,
        grid=(t // tm,),
        in_specs=[
            pl.BlockSpec((tm, d), lambda i: (i, 0)),
            pl.BlockSpec((None, 1, d), lambda i: (i // per_b, 0, 0)),
            pl.BlockSpec((None, 1, d), lambda i: (i // per_b, 0, 1)),
            pl.BlockSpec((1, d), lambda i: (0, 0)),
        ],
        out_specs=pl.BlockSpec((tm, d), lambda i: (i, 0)),
        out_shape=jax.ShapeDtypeStruct((t, d), F32),
        compiler_params=_params(("parallel",)),
        name="final_norm",
    )(x2, modf, modf, nw)


def kernel(x, c, w_ada, b_ada, norm1_w, w_in, conv_dw_w, conv_dw_b, conv_ln_w, conv_ln_b, w_pw2, conv_out_norm_w, qkv_conv_w, a_log, dt_bias, dn_norm_w, w_out, norm2_w, w_up, w_down, final_ada_w, final_ada_b, final_norm_w):
    batch, seq, d = x.shape
    depth = w_ada.shape[0]
    cc = conv_dw_w.shape[-1]
    heads = a_log.shape[-1]
    dh = dn_norm_w.shape[-1]
    dn = heads * dh
    n_main = 2 * cc + 4 * dn
    assert seq % CHUNK == 0 and dh % LANES == 0 and cc % LANES == 0
    assert w_in.shape[-1] == n_main + 2 * heads

    mods = _ada(c, w_ada, b_ada)
    modf = _ada(c, final_ada_w[None], final_ada_b[None])[0]
    modf = modf.reshape(batch, 1, 2 * d)

    x2 = x.reshape(batch * seq, d)
    for l in range(depth):
        mod4 = mods[l].reshape(batch, 1, 6 * d)
        w_main = w_in[l, :, :n_main].astype(BF)
        w_ba_t = w_in[l, :, n_main:].T.astype(BF)
        proj, ba_t = _inproj(x2, mod4, norm1_w[l].reshape(1, d), w_main, w_ba_t, seq)
        y_conv = _conv_group(proj, conv_dw_w[l], conv_dw_b[l], conv_ln_w[l], conv_ln_b[l],
                             w_pw2[l].astype(BF), conv_out_norm_w[l], batch, seq)
        y_dn = _delta_group(proj, ba_t, qkv_conv_w[l], a_log[l], dt_bias[l], dn_norm_w[l],
                            batch, seq, cc, heads, dh)
        x2 = _outproj(x2, y_conv, y_dn, w_out[l].astype(BF), mod4, seq)
        x2 = _mlp(x2, mod4, norm2_w[l].reshape(1, d), w_up[l].astype(BF), w_down[l].astype(BF), seq)
    out = _final(x2, modf, final_norm_w.reshape(1, d), seq)
    return out.reshape(batch, seq, d)
```

```python
import functools

import jax
import jax.numpy as jnp
from jax import lax
from jax.experimental import pallas as pl
from jax.experimental.pallas import tpu as pltpu

EPS = 1e-6
CHUNK = 64
SUBLANES = 8
LANES = 128
VMEM_LIMIT = 56 * 1024 * 1024
BF = jnp.bfloat16
F32 = jnp.float32


def _round_up(n, m):
    return (n + m - 1) // m * m


def _pick(n, pref):
    t = min(n, pref)
    while n % t:
        t //= 2
    return t


def _params(sem):
    return pltpu.CompilerParams(dimension_semantics=sem, vmem_limit_bytes=VMEM_LIMIT)


def _sigmoid(v):
    return jax.nn.sigmoid(v)


def _silu(v):
    return v * jax.nn.sigmoid(v)


def _ada_kernel(c_ref, w_ref, b_ref, o_ref):
    ca = _silu(c_ref[...]).astype(BF)
    o_ref[...] = jnp.dot(ca, w_ref[...].astype(BF), preferred_element_type=F32) + b_ref[...]


def _ada(c, w, b):
    nl, d, n = w.shape
    nb = c.shape[0]
    tn = _pick(n, 1024)
    return pl.pallas_call(
        _ada_kernel,
        grid=(nl, n // tn),
        in_specs=[
            pl.BlockSpec((nb, d), lambda l, j: (0, 0)),
            pl.BlockSpec((None, d, tn), lambda l, j: (l, 0, j)),
            pl.BlockSpec((None, 1, tn), lambda l, j: (l, 0, j)),
        ],
        out_specs=pl.BlockSpec((None, nb, tn), lambda l, j: (l, 0, j)),
        out_shape=jax.ShapeDtypeStruct((nl, nb, n), F32),
        compiler_params=_params(("parallel", "parallel")),
        name="ada",
    )(c, w, b.reshape(nl, 1, n))


def _norm_mod_rows(x_ref, nw_ref, shift_ref, scale_ref, h_ref, rows_per_step):
    tm = x_ref.shape[0]
    nw = nw_ref[...]
    mul = 1.0 + scale_ref[...]
    shift = shift_ref[...]

    def body(r, carry):
        r0 = pl.multiple_of(r * rows_per_step, rows_per_step)
        xr = x_ref[pl.ds(r0, rows_per_step), :]
        inv = lax.rsqrt(jnp.mean(xr * xr, axis=-1, keepdims=True) + EPS)
        h_ref[pl.ds(r0, rows_per_step), :] = ((xr * inv * nw) * mul + shift).astype(h_ref.dtype)
        return carry

    lax.fori_loop(0, tm // rows_per_step, body, 0)


def _inproj_kernel(x_ref, shift_ref, scale_ref, nw_ref, w_ref, wba_ref, proj_ref, ba_ref, h_ref):
    @pl.when(pl.program_id(1) == 0)
    def _():
        _norm_mod_rows(x_ref, nw_ref, shift_ref, scale_ref, h_ref, 16)
        ba_ref[...] = lax.dot_general(wba_ref[...], h_ref[...], (((1,), (1,)), ((), ())),
                                      preferred_element_type=F32)

    proj_ref[...] = jnp.dot(h_ref[...], w_ref[...], preferred_element_type=F32)


def _inproj(x2, mod4, nw, w_main, w_ba_t, seq):
    t, d = x2.shape
    n = w_main.shape[1]
    nba = w_ba_t.shape[0]
    tm = _pick(seq, 1024)
    tn = _pick(n, 1024)
    per_b = seq // tm
    return pl.pallas_call(
        _inproj_kernel,
        grid=(t // tm, n // tn),
        in_specs=[
            pl.BlockSpec((tm, d), lambda i, j: (i, 0)),
            pl.BlockSpec((None, 1, d), lambda i, j: (i // per_b, 0, 0)),
            pl.BlockSpec((None, 1, d), lambda i, j: (i // per_b, 0, 1)),
            pl.BlockSpec((1, d), lambda i, j: (0, 0)),
            pl.BlockSpec((d, tn), lambda i, j: (0, j)),
            pl.BlockSpec((nba, d), lambda i, j: (0, 0)),
        ],
        out_specs=[
            pl.BlockSpec((tm, tn), lambda i, j: (i, j)),
            pl.BlockSpec((nba, tm), lambda i, j: (0, i)),
        ],
        out_shape=[
            jax.ShapeDtypeStruct((t, n), F32),
            jax.ShapeDtypeStruct((nba, t), F32),
        ],
        scratch_shapes=[pltpu.VMEM((tm, d), BF)],
        compiler_params=_params(("parallel", "arbitrary")),
        name="inproj",
    )(x2, mod4, mod4, nw, w_main, w_ba_t)


def _conv_kernel(val_ref, gate_ref, dww_ref, dwb_ref, lnw_ref, lnb_ref, pw_ref, onw_ref,
                 y_ref, ubuf_ref, cbuf_ref, abuf_ref, *, kw, halo, rb, cr, cw):
    ts, cc = val_ref.shape

    @pl.when(pl.program_id(1) == 0)
    def _():
        ubuf_ref[pl.ds(0, halo), :] = jnp.zeros((halo, cc), F32)

    def glu(r, carry):
        r0 = pl.multiple_of(r * rb, rb)
        ubuf_ref[pl.ds(halo + r0, rb), :] = (
            val_ref[pl.ds(r0, rb), :] * _sigmoid(gate_ref[pl.ds(r0, rb), :]))
        return carry

    lax.fori_loop(0, ts // rb, glu, 0)

    lnw = lnw_ref[...]
    lnb = lnb_ref[...]

    first = halo - (kw - 1)
    for cb in range(cc // cw):
        cols = slice(cb * cw, (cb + 1) * cw)
        bias = dwb_ref[:, cols]

        def conv(r, carry, cols=cols, bias=bias):
            r0 = pl.multiple_of(r * cr, cr)
            win = ubuf_ref[pl.ds(r0, cr + halo), cols]
            acc = jnp.broadcast_to(bias, (cr, cw))
            for p in range(SUBLANES):
                offs = [o for o in range(first, halo + 1) if o % SUBLANES == p]
                if not offs:
                    continue
                wp = win if p == 0 else win[p:p + cr + halo - SUBLANES]
                for o in offs:
                    a = o - p
                    acc = acc + wp[a:a + cr] * dww_ref[pl.ds(o - first, 1), cols]
            cbuf_ref[pl.ds(r0, cr), cols] = acc
            return carry

        lax.fori_loop(0, ts // cr, conv, 0)

    def lnorm(r, carry):
        r0 = pl.multiple_of(r * rb, rb)
        acc = cbuf_ref[pl.ds(r0, rb), :]
        mu = jnp.mean(acc, axis=-1, keepdims=True)
        xc = acc - mu
        var = jnp.mean(xc * xc, axis=-1, keepdims=True)
        yn = xc * lax.rsqrt(var + EPS) * lnw + lnb
        abuf_ref[pl.ds(r0, rb), :] = _silu(yn).astype(BF)
        return carry

    lax.fori_loop(0, ts // rb, lnorm, 0)

    ubuf_ref[pl.ds(0, halo), :] = ubuf_ref[pl.ds(ts, halo), :]

    p = jnp.dot(abuf_ref[...], pw_ref[...], preferred_element_type=F32)
    inv = lax.rsqrt(jnp.mean(p * p, axis=-1, keepdims=True) + EPS)
    y_ref[...] = (p * inv * onw_ref[...]).astype(y_ref.dtype)


def _conv_group(proj, dww, dwb, lnw, lnb, pw, onw, batch, seq):
    kw, cc = dww.shape
    halo = _round_up(kw - 1, SUBLANES)
    ts = _pick(seq, 256)
    ns = seq // ts
    t = batch * seq
    row = lambda a: a.reshape(1, cc)
    return pl.pallas_call(
        functools.partial(_conv_kernel, kw=kw, halo=halo, rb=16, cr=_pick(ts, 64), cw=_pick(cc, 256)),
        grid=(batch, ns),
        in_specs=[
            pl.BlockSpec((ts, cc), lambda b, s: (b * ns + s, 0)),
            pl.BlockSpec((ts, cc), lambda b, s: (b * ns + s, 1)),
            pl.BlockSpec((kw, cc), lambda b, s: (0, 0)),
            pl.BlockSpec((1, cc), lambda b, s: (0, 0)),
            pl.BlockSpec((1, cc), lambda b, s: (0, 0)),
            pl.BlockSpec((1, cc), lambda b, s: (0, 0)),
            pl.BlockSpec((cc, cc), lambda b, s: (0, 0)),
            pl.BlockSpec((1, cc), lambda b, s: (0, 0)),
        ],
        out_specs=pl.BlockSpec((ts, cc), lambda b, s: (b * ns + s, 0)),
        out_shape=jax.ShapeDtypeStruct((t, cc), BF),
        scratch_shapes=[
            pltpu.VMEM((halo + ts, cc), F32),
            pltpu.VMEM((ts, cc), F32),
            pltpu.VMEM((ts, cc), BF),
        ],
        compiler_params=_params(("arbitrary", "arbitrary")),
        name="conv_group",
    )(proj, proj, dww, row(dwb), row(lnw), row(lnb), pw, row(onw))


def _dot_nt(a, b):
    return lax.dot_general(a, b, (((1,), (1,)), ((), ())), preferred_element_type=F32)


def _dot(a, b):
    return jnp.dot(a, b, preferred_element_type=F32)


def _hi_lo(x):
    hi = x.astype(BF).astype(F32)
    return hi, x - hi


def _blockdiag(xp, same_chunk):
    g = xp.shape[1] // CHUNK
    return jnp.where(same_chunk, jnp.concatenate([xp] * g, axis=0), 0.0).astype(BF)


def _packed_mm(xs, y, same_chunk):
    yh, yl = _hi_lo(y)
    yh_bd = _blockdiag(yh, same_chunk)
    yl_bd = _blockdiag(yl, same_chunk)
    his, los = zip(*[_hi_lo(x) for x in xs])
    n = len(xs)
    a = _dot(jnp.concatenate(his + los, axis=0).astype(BF), yh_bd)
    b = _dot(jnp.concatenate(his, axis=0).astype(BF), yl_bd)
    c = CHUNK
    return [a[i * c:(i + 1) * c] + a[(n + i) * c:(n + i + 1) * c] + b[i * c:(i + 1) * c] for i in range(n)]


def _packed_unit_lower_inverse(lps, same_chunk):
    ts = lps[0].shape[1]
    pi = lax.broadcasted_iota(jnp.int32, (CHUNK, ts), 0)
    pj = lax.broadcasted_iota(jnp.int32, (CHUNK, ts), 1) % CHUNK
    eye = (pi == pj).astype(F32)
    same16 = (pi // 16) == (pj // 16)
    same32 = (pi // 32) == (pj // 32)
    mm = lambda xs_list, ys: [_packed_mm(xs, y, same_chunk) for xs, y in zip(xs_list, ys)]

    l16 = [jnp.where(same16, lp, 0.0) for lp in lps]
    l2 = [r[0] for r in mm([[a] for a in l16], l16)]
    x = [eye - a for a in l16]
    r = mm([[a, b] for a, b in zip(x, l2)], l2)
    x = [a + b[0] for a, b in zip(x, r)]
    l4 = [b[1] for b in r]
    r = mm([[a, b] for a, b in zip(x, l4)], l4)
    x = [a + b[0] for a, b in zip(x, r)]
    l8 = [b[1] for b in r]
    r = mm([[a] for a in x], l8)
    x = [a + b[0] for a, b in zip(x, r)]
    for keep in (same32 & jnp.logical_not(same16), jnp.logical_not(same32)):
        cm = [jnp.where(keep, lp, 0.0) for lp in lps]
        t1 = [b[0] for b in mm([[a] for a in x], cm)]
        t2 = [b[0] for b in mm([[a] for a in t1], x)]
        x = [a - b for a, b in zip(x, t2)]
    return x


def _delta_kernel(alog_ref, dtb_ref, q_ref, k_ref, v_ref, z_ref, ba_ref, wq_ref, wk_ref, wv_ref,
                  nw_ref, y_ref, state_ref, halo_ref, ext_ref, cols_ref, lp_ref, attn_ref, rhs_ref,
                  qd_ref, kdt_ref, lhs_ref, n_ref, o0_ref, *, heads, hg, sc):
    ts, width = q_ref.shape
    dh = width // hg
    g = ts // CHUNK
    head0 = pl.program_id(1) * hg
    hrows = halo_ref.shape[1]

    @pl.when(pl.program_id(2) == 0)
    def _():
        state_ref[...] = jnp.zeros_like(state_ref)
        halo_ref[...] = jnp.zeros_like(halo_ref)

    zrow = jnp.zeros((SUBLANES - hg, ts), F32)
    betas, gs = [], []
    for i in range(hg):
        b_logit = ba_ref[pl.ds(head0 + i, 1), :]
        xs = ba_ref[pl.ds(heads + head0 + i, 1), :] + dtb_ref[head0 + i]
        softplus = jnp.maximum(xs, 0.0) + jnp.log(1.0 + jnp.exp(-jnp.abs(xs)))
        betas.append(_sigmoid(b_logit))
        gs.append(-jnp.exp(jnp.full((1, ts), alog_ref[head0 + i], F32)) * softplus)
    beta_rows = jnp.concatenate(betas + [zrow], axis=0)
    g_rows = jnp.concatenate(gs + [zrow], axis=0)
    lane = lax.broadcasted_iota(jnp.int32, (SUBLANES, ts), 1) % CHUNK
    d = 1
    while d < CHUNK:
        g_rows = g_rows + jnp.where(lane >= d, pltpu.roll(g_rows, d, 1), 0.0)
        d *= 2
    gl_rows = jnp.where(lane == CHUNK - 1, g_rows, 0.0)
    d = 1
    while d < CHUNK:
        gl_rows = gl_rows + jnp.where(lane + d < CHUNK, pltpu.roll(gl_rows, ts - d, 1), 0.0)
        d *= 2
    kdec_rows = jnp.exp(gl_rows - g_rows)
    rows = jnp.concatenate([beta_rows, g_rows, jnp.zeros((LANES - 2 * SUBLANES, ts), F32)], axis=0)
    cols_ref[...] = rows.T

    ri = lax.broadcasted_iota(jnp.int32, (ts, ts), 0)
    ci = lax.broadcasted_iota(jnp.int32, (ts, ts), 1)
    same_chunk = (ri // CHUNK) == (ci // CHUNK)
    causal = same_chunk & (ri >= ci)

    def short_conv(idx, x_ref, w_ref, lanes):
        ext_ref[pl.ds(0, hrows), :] = halo_ref[idx, :, lanes]
        ext_ref[pl.ds(hrows, ts), :] = x_ref[:, lanes]
        acc = jnp.zeros((ts, dh), F32)
        for j in range(sc):
            acc = acc + ext_ref[pl.ds(hrows - (sc - 1) + j, ts), :] * w_ref[pl.ds(j, 1), lanes]
        halo_ref[idx, :, lanes] = x_ref[pl.ds(ts - hrows, hrows), lanes]
        return _silu(acc)

    for i in range(hg):
        lanes = slice(i * dh, (i + 1) * dh)
        q = short_conv(0, q_ref, wq_ref, lanes)
        k = short_conv(1, k_ref, wk_ref, lanes)
        v = short_conv(2, v_ref, wv_ref, lanes)
        q = q * lax.rsqrt(jnp.sum(q * q, axis=-1, keepdims=True) + EPS) * (dh ** -0.5)
        k = k * lax.rsqrt(jnp.sum(k * k, axis=-1, keepdims=True) + EPS)
        beta_col = cols_ref[:, i:i + 1]
        g_col = cols_ref[:, SUBLANES + i:SUBLANES + i + 1]
        g_row = g_rows[i:i + 1]
        eg = jnp.exp(g_col)
        kb = k * beta_col
        rhs_ref[i] = jnp.concatenate([v * beta_col, kb * eg], axis=1)
        qd_ref[i] = q * eg
        decay = jnp.exp(jnp.where(causal, g_col - g_row, -jnp.inf))
        kq = _dot_nt(jnp.concatenate([kb, q], axis=0).astype(BF), k.astype(BF))
        attn_ref[i] = (kq[ts:] * decay).astype(BF)
        lfull = jnp.where(ri > ci, kq[:ts] * decay, 0.0)
        lp = lfull[0:CHUNK]
        for c in range(1, g):
            lp = lp + lfull[c * CHUNK:(c + 1) * CHUNK]
        lp_ref[i] = lp
        kdt_ref[i] = k.T * kdec_rows[i:i + 1]

    tinv = _packed_unit_lower_inverse([lp_ref[i] for i in range(hg)], same_chunk)

    cj = lax.broadcasted_iota(jnp.int32, (dh, ts), 1) // CHUNK
    for i in range(hg):
        th, tl = _hi_lo(tinv[i])
        th_bd = _blockdiag(th, same_chunk)
        tl_bd = _blockdiag(tl, same_chunk)
        rh, rl = _hi_lo(rhs_ref[i])
        sol = _dot(th_bd, jnp.concatenate([rh, rl], axis=1).astype(BF))
        uw = sol[:, :2 * dh] + sol[:, 2 * dh:] + _dot(tl_bd, rh.astype(BF))
        uw16 = uw.astype(BF)
        auw = _dot(attn_ref[i], uw16)
        o0_ref[i] = auw[:, :dh]
        qp16 = (qd_ref[i] - auw[:, dh:]).astype(BF)
        kdt = kdt_ref[i]
        for c in range(g):
            kun = _dot(jnp.where(cj == c, kdt, 0.0).astype(BF), uw16)
            n_ref[i, c] = kun[:, :dh]
            lhs_ref[i, c] = jnp.concatenate(
                [kun[:, dh:].astype(BF), qp16[c * CHUNK:(c + 1) * CHUNK]], axis=0)

    nw = nw_ref[...]
    for c in range(g):
        r0 = c * CHUNK
        for i in range(hg):
            lanes = slice(i * dh, (i + 1) * dh)
            state = state_ref[i]
            r = _dot(lhs_ref[i, c], state.astype(BF))
            o = r[dh:] + o0_ref[i, pl.ds(r0, CHUNK), :]
            tail = cols_ref[pl.ds(r0 + CHUNK - SUBLANES, SUBLANES), :]
            g_last = tail[SUBLANES - 1:SUBLANES, SUBLANES + i:SUBLANES + i + 1]
            state_ref[i] = state * jnp.exp(g_last) - r[:dh] + n_ref[i, c]
            on = o * lax.rsqrt(jnp.mean(o * o, axis=-1, keepdims=True) + EPS) * nw
            y_ref[pl.ds(r0, CHUNK), lanes] = (on * _silu(z_ref[pl.ds(r0, CHUNK), lanes])).astype(y_ref.dtype)


def _delta_group(proj, ba_t, conv_w, a_log, dt_bias, nw, batch, seq, cc, heads, dh):
    sc = conv_w.shape[0]
    dn = heads * dh
    hg = _pick(heads, 4)
    width = hg * dh
    assert (2 * cc) % width == 0 and hg <= SUBLANES
    ts = _pick(seq, 256)
    ns = seq // ts
    g = ts // CHUNK
    t = batch * seq
    nba = ba_t.shape[0]
    hrows = _round_up(sc - 1, SUBLANES)
    tok = lambda col0: pl.BlockSpec((ts, width), lambda b, h, s: (b * ns + s, col0 // width + h))
    cw = lambda col0: pl.BlockSpec((sc, width), lambda b, h, s: (0, col0 // width + h))
    smem = pl.BlockSpec(memory_space=pltpu.SMEM)
    return pl.pallas_call(
        functools.partial(_delta_kernel, heads=heads, hg=hg, sc=sc),
        grid=(batch, heads // hg, ns),
        in_specs=[
            smem, smem,
            tok(2 * cc), tok(2 * cc + dn), tok(2 * cc + 2 * dn), tok(2 * cc + 3 * dn),
            pl.BlockSpec((nba, ts), lambda b, h, s: (0, b * ns + s)),
            cw(0), cw(dn), cw(2 * dn),
            pl.BlockSpec((1, dh), lambda b, h, s: (0, 0)),
        ],
        out_specs=pl.BlockSpec((ts, width), lambda b, h, s: (b * ns + s, h)),
        out_shape=jax.ShapeDtypeStruct((t, dn), BF),
        scratch_shapes=[
            pltpu.VMEM((hg, dh, dh), F32),
            pltpu.VMEM((3, hrows, width), F32),
            pltpu.VMEM((hrows + ts, dh), F32),
            pltpu.VMEM((ts, LANES), F32),
            pltpu.VMEM((hg, CHUNK, ts), F32),
            pltpu.VMEM((hg, ts, ts), BF),
            pltpu.VMEM((hg, ts, 2 * dh), F32),
            pltpu.VMEM((hg, ts, dh), F32),
            pltpu.VMEM((hg, dh, ts), F32),
            pltpu.VMEM((hg, g, dh + CHUNK, dh), BF),
            pltpu.VMEM((hg, g, dh, dh), F32),
            pltpu.VMEM((hg, ts, dh), F32),
        ],
        compiler_params=_params(("arbitrary", "arbitrary", "arbitrary")),
        name="delta_group",
    )(a_log, dt_bias, proj, proj, proj, proj, ba_t, conv_w, conv_w, conv_w, nw.reshape(1, dh))


def _outproj_kernel(x_ref, yc_ref, yd_ref, wc_ref, wd_ref, gate_ref, o_ref):
    y = (jnp.dot(yc_ref[...], wc_ref[...], preferred_element_type=F32)
         + jnp.dot(yd_ref[...], wd_ref[...], preferred_element_type=F32))
    o_ref[...] = x_ref[...] + gate_ref[...] * y


def _outproj(x2, y_conv, y_dn, w_out, mod4, seq):
    t, d = x2.shape
    cc = y_conv.shape[1]
    dn = y_dn.shape[1]
    tm = _pick(seq, 512)
    per_b = seq // tm
    return pl.pallas_call(
        _outproj_kernel,
        grid=(t // tm,),
        in_specs=[
            pl.BlockSpec((tm, d), lambda i: (i, 0)),
            pl.BlockSpec((tm, cc), lambda i: (i, 0)),
            pl.BlockSpec((tm, dn), lambda i: (i, 0)),
            pl.BlockSpec((cc, d), lambda i: (0, 0)),
            pl.BlockSpec((dn, d), lambda i: (0, 0)),
            pl.BlockSpec((None, 1, d), lambda i: (i // per_b, 0, 2)),
        ],
        out_specs=pl.BlockSpec((tm, d), lambda i: (i, 0)),
        out_shape=jax.ShapeDtypeStruct((t, d), F32),
        compiler_params=_params(("parallel",)),
        name="outproj",
    )(x2, y_conv, y_dn, w_out[:cc], w_out[cc:], mod4)


def _mlp_kernel(x_ref, shift_ref, scale_ref, gate_ref, nw_ref, wu_ref, wd_ref, o_ref, h_ref):
    j = pl.program_id(1)

    @pl.when(j == 0)
    def _():
        _norm_mod_rows(x_ref, nw_ref, shift_ref, scale_ref, h_ref, 16)

    a = jnp.maximum(jnp.dot(h_ref[...], wu_ref[...], preferred_element_type=F32), 0.0)
    m = jnp.dot((a * a).astype(BF), wd_ref[...], preferred_element_type=F32)

    @pl.when(j == 0)
    def _():
        o_ref[...] = m

    @pl.when(j > 0)
    def _():
        o_ref[...] += m

    @pl.when(j == pl.num_programs(1) - 1)
    def _():
        o_ref[...] = x_ref[...] + gate_ref[...] * o_ref[...]


def _mlp(x2, mod4, nw, w_up, w_down, seq):
    t, d = x2.shape
    f = w_up.shape[1]
    tm = _pick(seq, 1024)
    tf = _pick(f, 512)
    per_b = seq // tm
    modspec = lambda which: pl.BlockSpec((None, 1, d), lambda i, j: (i // per_b, 0, which))
    return pl.pallas_call(
        _mlp_kernel,
        grid=(t // tm, f // tf),
        in_specs=[
            pl.BlockSpec((tm, d), lambda i, j: (i, 0)),
            modspec(3), modspec(4), modspec(5),
            pl.BlockSpec((1, d), lambda i, j: (0, 0)),
            pl.BlockSpec((d, tf), lambda i, j: (0, j)),
            pl.BlockSpec((tf, d), lambda i, j: (j, 0)),
        ],
        out_specs=pl.BlockSpec((tm, d), lambda i, j: (i, 0)),
        out_shape=jax.ShapeDtypeStruct((t, d), F32),
        scratch_shapes=[pltpu.VMEM((tm, d), BF)],
        compiler_params=_params(("parallel", "arbitrary")),
        name="mlp",
    )(x2, mod4, mod4, mod4, nw, w_up, w_down)


def _final_kernel(x_ref, shift_ref, scale_ref, nw_ref, o_ref):
    _norm_mod_rows(x_ref, nw_ref, shift_ref, scale_ref, o_ref, 16)


def _final(x2, modf, nw, seq):
    t, d = x2.shape
    tm = _pick(seq, 512)
    per_b = seq // tm
    return pl.pallas_call(
        _final_kernel,
        grid=(t // tm,),
        in_specs=[
            pl.BlockSpec((tm, d), lambda i: (i, 0)),
            pl.BlockSpec((None, 1, d), lambda i: (i // per_b, 0, 0)),
            pl.BlockSpec((None, 1, d), lambda i: (i // per_b, 0, 1)),
            pl.BlockSpec((1, d), lambda i: (0, 0)),
        ],
        out_specs=pl.BlockSpec((tm, d), lambda i: (i, 0)),
        out_shape=jax.ShapeDtypeStruct((t, d), F32),
        compiler_params=_params(("parallel",)),
        name="final_norm",
    )(x2, modf, modf, nw)


def kernel(x, c, w_ada, b_ada, norm1_w, w_in, conv_dw_w, conv_dw_b, conv_ln_w, conv_ln_b, w_pw2, conv_out_norm_w, qkv_conv_w, a_log, dt_bias, dn_norm_w, w_out, norm2_w, w_up, w_down, final_ada_w, final_ada_b, final_norm_w):
    batch, seq, d = x.shape
    depth = w_ada.shape[0]
    cc = conv_dw_w.shape[-1]
    heads = a_log.shape[-1]
    dh = dn_norm_w.shape[-1]
    dn = heads * dh
    n_main = 2 * cc + 4 * dn
    assert seq % CHUNK == 0 and dh % LANES == 0 and cc % LANES == 0
    assert w_in.shape[-1] == n_main + 2 * heads

    mods = _ada(c, w_ada, b_ada)
    modf = _ada(c, final_ada_w[None], final_ada_b[None])[0]
    modf = modf.reshape(batch, 1, 2 * d)

    x2 = x.reshape(batch * seq, d)
    for l in range(depth):
        mod4 = mods[l].reshape(batch, 1, 6 * d)
        w_main = w_in[l, :, :n_main].astype(BF)
        w_ba_t = w_in[l, :, n_main:].T.astype(BF)
        proj, ba_t = _inproj(x2, mod4, norm1_w[l].reshape(1, d), w_main, w_ba_t, seq)
        y_conv = _conv_group(proj, conv_dw_w[l], conv_dw_b[l], conv_ln_w[l], conv_ln_b[l],
                             w_pw2[l].astype(BF), conv_out_norm_w[l], batch, seq)
        y_dn = _delta_group(proj, ba_t, qkv_conv_w[l], a_log[l], dt_bias[l], dn_norm_w[l],
                            batch, seq, cc, heads, dh)
        x2 = _outproj(x2, y_conv, y_dn, w_out[l].astype(BF), mod4, seq)
        x2 = _mlp(x2, mod4, norm2_w[l].reshape(1, d), w_up[l].astype(BF), w_down[l].astype(BF), seq)
    out = _final(x2, modf, final_norm_w.reshape(1, d), seq)
    return out.reshape(batch, seq, d)
```

```python
import functools

import jax
import jax.numpy as jnp
from jax import lax
from jax.experimental import pallas as pl
from jax.experimental.pallas import tpu as pltpu

EPS = 1e-6
CHUNK = 64
SUBLANES = 8
LANES = 128
VMEM_LIMIT = 56 * 1024 * 1024
BF = jnp.bfloat16
F32 = jnp.float32


def _round_up(n, m):
    return (n + m - 1) // m * m


def _pick(n, pref):
    t = min(n, pref)
    while n % t:
        t //= 2
    return t


def _params(sem):
    return pltpu.CompilerParams(dimension_semantics=sem, vmem_limit_bytes=VMEM_LIMIT)


def _sigmoid(v):
    return jax.nn.sigmoid(v)


def _silu(v):
    return v * jax.nn.sigmoid(v)


def _ada_kernel(c_ref, w_ref, b_ref, o_ref):
    ca = _silu(c_ref[...]).astype(BF)
    o_ref[...] = jnp.dot(ca, w_ref[...].astype(BF), preferred_element_type=F32) + b_ref[...]


def _ada(c, w, b):
    nl, d, n = w.shape
    nb = c.shape[0]
    tn = _pick(n, 1024)
    return pl.pallas_call(
        _ada_kernel,
        grid=(nl, n // tn),
        in_specs=[
            pl.BlockSpec((nb, d), lambda l, j: (0, 0)),
            pl.BlockSpec((None, d, tn), lambda l, j: (l, 0, j)),
            pl.BlockSpec((None, 1, tn), lambda l, j: (l, 0, j)),
        ],
        out_specs=pl.BlockSpec((None, nb, tn), lambda l, j: (l, 0, j)),
        out_shape=jax.ShapeDtypeStruct((nl, nb, n), F32),
        compiler_params=_params(("parallel", "parallel")),
        name="ada",
    )(c, w, b.reshape(nl, 1, n))


def _norm_mod_rows(x_ref, nw_ref, shift_ref, scale_ref, h_ref, rows_per_step):
    tm = x_ref.shape[0]
    gain = nw_ref[...] * (1.0 + scale_ref[...])
    shift = shift_ref[...]
    for r in range(tm // rows_per_step):
        rows = pl.ds(r * rows_per_step, rows_per_step)
        xr = x_ref[rows, :]
        inv = lax.rsqrt(jnp.mean(xr * xr, axis=-1, keepdims=True) + EPS)
        h_ref[rows, :] = (x_ref[rows, :] * inv * gain + shift).astype(h_ref.dtype)


def _inproj_kernel(x_ref, shift_ref, scale_ref, nw_ref, w_ref, wba_ref, proj_ref, ba_ref, h_ref):
    @pl.when(pl.program_id(1) == 0)
    def _():
        _norm_mod_rows(x_ref, nw_ref, shift_ref, scale_ref, h_ref, 16)
        ba_ref[...] = lax.dot_general(wba_ref[...], h_ref[...], (((1,), (1,)), ((), ())),
                                      preferred_element_type=F32)

    proj_ref[...] = jnp.dot(h_ref[...], w_ref[...], preferred_element_type=F32)


def _inproj(x2, mod4, nw, w_in16, w_ba_t, layer, n, seq):
    t, d = x2.shape
    nba = w_ba_t.shape[0]
    tm = _pick(seq, 1024)
    tn = _pick(n, 1024)
    per_b = seq // tm
    return pl.pallas_call(
        _inproj_kernel,
        grid=(t // tm, n // tn),
        in_specs=[
            pl.BlockSpec((tm, d), lambda i, j: (i, 0)),
            pl.BlockSpec((None, 1, d), lambda i, j: (i // per_b, 0, 0)),
            pl.BlockSpec((None, 1, d), lambda i, j: (i // per_b, 0, 1)),
            pl.BlockSpec((1, d), lambda i, j: (0, 0)),
            pl.BlockSpec((None, d, tn), lambda i, j: (layer, 0, j)),
            pl.BlockSpec((nba, d), lambda i, j: (0, 0)),
        ],
        out_specs=[
            pl.BlockSpec((tm, tn), lambda i, j: (i, j)),
            pl.BlockSpec((nba, tm), lambda i, j: (0, i)),
        ],
        out_shape=[
            jax.ShapeDtypeStruct((t, n), F32),
            jax.ShapeDtypeStruct((nba, t), F32),
        ],
        scratch_shapes=[pltpu.VMEM((tm, d), BF)],
        compiler_params=_params(("parallel", "arbitrary")),
        name="inproj",
    )(x2, mod4, mod4, nw, w_in16, w_ba_t)


def _conv_kernel(val_ref, gate_ref, dww_ref, dwb_ref, lnw_ref, lnb_ref, pw_ref, onw_ref,
                 y_ref, ubuf_ref, cbuf_ref, abuf_ref, *, kw, halo, rb, cr):
    ts, cc = val_ref.shape
    ncol = cc // LANES

    @pl.when(pl.program_id(1) == 0)
    def _():
        ubuf_ref[:, pl.ds(0, halo), :] = jnp.zeros((ncol, halo, LANES), F32)

    def glu(r, carry):
        r0 = pl.multiple_of(r * rb, rb)
        for cb in range(ncol):
            lanes = slice(cb * LANES, (cb + 1) * LANES)
            ubuf_ref[cb, pl.ds(halo + r0, rb), :] = (
                val_ref[pl.ds(r0, rb), lanes] * _sigmoid(gate_ref[pl.ds(r0, rb), lanes]))
        return carry

    lax.fori_loop(0, ts // rb, glu, 0)

    lnw = lnw_ref[...]
    lnb = lnb_ref[...]

    first = halo - (kw - 1)
    for cb in range(ncol):
        lanes = slice(cb * LANES, (cb + 1) * LANES)
        for r0 in range(0, ts, cr):
            acc = jnp.broadcast_to(dwb_ref[:, lanes], (cr, LANES))
            for k in range(kw):
                acc = acc + ubuf_ref[cb, pl.ds(r0 + first + k, cr), :] * dww_ref[pl.ds(k, 1), lanes]
            cbuf_ref[pl.ds(r0, cr), lanes] = acc

    for r in range(ts // rb):
        rows = pl.ds(r * rb, rb)
        acc = cbuf_ref[rows, :]
        mu = jnp.mean(acc, axis=-1, keepdims=True)
        xc = acc - mu
        var = jnp.mean(xc * xc, axis=-1, keepdims=True)
        yn = xc * lax.rsqrt(var + EPS) * lnw + lnb
        abuf_ref[rows, :] = _silu(yn).astype(BF)

    ubuf_ref[:, pl.ds(0, halo), :] = ubuf_ref[:, pl.ds(ts, halo), :]

    p = jnp.dot(abuf_ref[...], pw_ref[...], preferred_element_type=F32)
    inv = lax.rsqrt(jnp.mean(p * p, axis=-1, keepdims=True) + EPS)
    y_ref[...] = (p * inv * onw_ref[...]).astype(y_ref.dtype)


def _conv_group(proj, dww, dwb, lnw, lnb, pw16, layer, onw, batch, seq):
    kw, cc = dww.shape
    halo = _round_up(kw - 1, SUBLANES)
    ts = _pick(seq, 256)
    ns = seq // ts
    t = batch * seq
    row = lambda a: a.reshape(1, cc)
    return pl.pallas_call(
        functools.partial(_conv_kernel, kw=kw, halo=halo, rb=16, cr=_pick(ts, 64)),
        grid=(batch, ns),
        in_specs=[
            pl.BlockSpec((ts, cc), lambda b, s: (b * ns + s, 0)),
            pl.BlockSpec((ts, cc), lambda b, s: (b * ns + s, 1)),
            pl.BlockSpec((kw, cc), lambda b, s: (0, 0)),
            pl.BlockSpec((1, cc), lambda b, s: (0, 0)),
            pl.BlockSpec((1, cc), lambda b, s: (0, 0)),
            pl.BlockSpec((1, cc), lambda b, s: (0, 0)),
            pl.BlockSpec((None, cc, cc), lambda b, s: (layer, 0, 0)),
            pl.BlockSpec((1, cc), lambda b, s: (0, 0)),
        ],
        out_specs=pl.BlockSpec((ts, cc), lambda b, s: (b * ns + s, 0)),
        out_shape=jax.ShapeDtypeStruct((t, cc), BF),
        scratch_shapes=[
            pltpu.VMEM((cc // LANES, halo + ts, LANES), F32),
            pltpu.VMEM((ts, cc), F32),
            pltpu.VMEM((ts, cc), BF),
        ],
        compiler_params=_params(("arbitrary", "arbitrary")),
        name="conv_group",
    )(proj, proj, dww, row(dwb), row(lnw), row(lnb), pw16, row(onw))


def _dot_nt(a, b):
    return lax.dot_general(a, b, (((1,), (1,)), ((), ())), preferred_element_type=F32)


def _dot(a, b):
    return jnp.dot(a, b, preferred_element_type=F32)


def _blockdiag(xp, same_chunk):
    g = xp.shape[1] // CHUNK
    return jnp.where(same_chunk, jnp.concatenate([xp] * g, axis=0), 0.0).astype(BF)


def _packed_mm(xs, y, same_chunk):
    a = _dot(jnp.concatenate(list(xs), axis=0).astype(BF), _blockdiag(y, same_chunk))
    return [a[i * CHUNK:(i + 1) * CHUNK] for i in range(len(xs))]


def _packed_unit_lower_inverse(lps, same_chunk):
    ts = lps[0].shape[1]
    pi = lax.broadcasted_iota(jnp.int32, (CHUNK, ts), 0)
    pj = lax.broadcasted_iota(jnp.int32, (CHUNK, ts), 1) % CHUNK
    eye = (pi == pj).astype(F32)
    same16 = (pi // 16) == (pj // 16)
    same32 = (pi // 32) == (pj // 32)
    mm = lambda xs_list, ys: [_packed_mm(xs, y, same_chunk) for xs, y in zip(xs_list, ys)]

    l16 = [jnp.where(same16, lp, 0.0) for lp in lps]
    l2 = [r[0] for r in mm([[a] for a in l16], l16)]
    x = [eye - a for a in l16]
    r = mm([[a, b] for a, b in zip(x, l2)], l2)
    x = [a + b[0] for a, b in zip(x, r)]
    l4 = [b[1] for b in r]
    r = mm([[a, b] for a, b in zip(x, l4)], l4)
    x = [a + b[0] for a, b in zip(x, r)]
    l8 = [b[1] for b in r]
    r = mm([[a] for a in x], l8)
    x = [a + b[0] for a, b in zip(x, r)]
    for keep in (same32 & jnp.logical_not(same16), jnp.logical_not(same32)):
        cm = [jnp.where(keep, lp, 0.0) for lp in lps]
        t1 = [b[0] for b in mm([[a] for a in x], cm)]
        t2 = [b[0] for b in mm([[a] for a in t1], x)]
        x = [a - b for a, b in zip(x, t2)]
    return x


def _delta_kernel(alog_ref, dtb_ref, q_ref, k_ref, v_ref, z_ref, ba_ref, wq_ref, wk_ref, wv_ref,
                  nw_ref, y_ref, state_ref, halo_ref, ext_ref, cols_ref, lp_ref, attn_ref, rhs_ref,
                  qd_ref, kdt_ref, lhs_ref, n_ref, o0_ref, *, heads, hg, sc):
    ts, width = q_ref.shape
    dh = width // hg
    g = ts // CHUNK
    head0 = pl.program_id(1) * hg
    hrows = halo_ref.shape[1]

    @pl.when(pl.program_id(2) == 0)
    def _():
        state_ref[...] = jnp.zeros_like(state_ref)
        halo_ref[...] = jnp.zeros_like(halo_ref)

    pad = [jnp.zeros((SUBLANES - hg, ts), F32)] if hg < SUBLANES else []
    betas, gs = [], []
    for i in range(hg):
        b_logit = ba_ref[pl.ds(head0 + i, 1), :]
        xs = ba_ref[pl.ds(heads + head0 + i, 1), :] + dtb_ref[head0 + i]
        softplus = jnp.maximum(xs, 0.0) + jnp.log(1.0 + jnp.exp(-jnp.abs(xs)))
        betas.append(_sigmoid(b_logit))
        gs.append(-jnp.exp(jnp.full((1, ts), alog_ref[head0 + i], F32)) * softplus)
    beta_rows = jnp.concatenate(betas + pad, axis=0)
    g_rows = jnp.concatenate(gs + pad, axis=0)
    lane = lax.broadcasted_iota(jnp.int32, (SUBLANES, ts), 1) % CHUNK
    d = 1
    while d < CHUNK:
        g_rows = g_rows + jnp.where(lane >= d, pltpu.roll(g_rows, d, 1), 0.0)
        d *= 2
    gl_rows = jnp.where(lane == CHUNK - 1, g_rows, 0.0)
    d = 1
    while d < CHUNK:
        gl_rows = gl_rows + jnp.where(lane + d < CHUNK, pltpu.roll(gl_rows, ts - d, 1), 0.0)
        d *= 2
    kdec_rows = jnp.exp(gl_rows - g_rows)
    rows = jnp.concatenate([beta_rows, g_rows, jnp.zeros((LANES - 2 * SUBLANES, ts), F32)], axis=0)
    cols_ref[...] = rows.T

    ri = lax.broadcasted_iota(jnp.int32, (ts, ts), 0)
    ci = lax.broadcasted_iota(jnp.int32, (ts, ts), 1)
    same_chunk = (ri // CHUNK) == (ci // CHUNK)
    causal = same_chunk & (ri >= ci)

    def short_conv(idx, x_ref, w_ref, lanes):
        ext_ref[pl.ds(0, hrows), :] = halo_ref[idx, :, lanes]
        ext_ref[pl.ds(hrows, ts), :] = x_ref[:, lanes]
        acc = jnp.zeros((ts, dh), F32)
        for j in range(sc):
            acc = acc + ext_ref[pl.ds(hrows - (sc - 1) + j, ts), :] * w_ref[pl.ds(j, 1), lanes]
        halo_ref[idx, :, lanes] = x_ref[pl.ds(ts - hrows, hrows), lanes]
        return _silu(acc)

    for i in range(hg):
        lanes = slice(i * dh, (i + 1) * dh)
        q = short_conv(0, q_ref, wq_ref, lanes)
        k = short_conv(1, k_ref, wk_ref, lanes)
        v = short_conv(2, v_ref, wv_ref, lanes)
        q = q * lax.rsqrt(jnp.sum(q * q, axis=-1, keepdims=True) + EPS) * (dh ** -0.5)
        k = k * lax.rsqrt(jnp.sum(k * k, axis=-1, keepdims=True) + EPS)
        beta_col = cols_ref[:, i:i + 1]
        g_col = cols_ref[:, SUBLANES + i:SUBLANES + i + 1]
        g_row = g_rows[i:i + 1]
        eg = jnp.exp(g_col)
        kb = k * beta_col
        rhs_ref[i] = jnp.concatenate([v * beta_col, kb * eg], axis=1)
        qd_ref[i] = q * eg
        decay = jnp.exp(jnp.where(causal, g_col - g_row, -jnp.inf))
        kq = _dot_nt(jnp.concatenate([kb, q], axis=0).astype(BF), k.astype(BF))
        attn_ref[i] = (kq[ts:] * decay).astype(BF)
        lfull = jnp.where(ri > ci, kq[:ts] * decay, 0.0)
        lp = lfull[0:CHUNK]
        for c in range(1, g):
            lp = lp + lfull[c * CHUNK:(c + 1) * CHUNK]
        lp_ref[i] = lp
        kdt_ref[i] = k.T * kdec_rows[i:i + 1]

    tinv = _packed_unit_lower_inverse([lp_ref[i] for i in range(hg)], same_chunk)

    cj = lax.broadcasted_iota(jnp.int32, (dh, ts), 1) // CHUNK
    eye_p = (lax.broadcasted_iota(jnp.int32, (CHUNK, ts), 0)
             == lax.broadcasted_iota(jnp.int32, (CHUNK, ts), 1) % CHUNK).astype(F32)
    for i in range(hg):
        t_off = _blockdiag(tinv[i] - eye_p, same_chunk)
        rhs = rhs_ref[i]
        uw = rhs + _dot(t_off, rhs.astype(BF))
        uw16 = uw.astype(BF)
        auw = _dot(attn_ref[i], uw16)
        o0_ref[i] = auw[:, :dh]
        qp16 = (qd_ref[i] - auw[:, dh:]).astype(BF)
        kdt = kdt_ref[i]
        for c in range(g):
            kun = _dot(jnp.where(cj == c, kdt, 0.0).astype(BF), uw16)
            n_ref[i, c] = kun[:, :dh]
            lhs_ref[i, c] = jnp.concatenate(
                [kun[:, dh:].astype(BF), qp16[c * CHUNK:(c + 1) * CHUNK]], axis=0)

    nw = nw_ref[...]
    for c in range(g):
        r0 = c * CHUNK
        for i in range(hg):
            lanes = slice(i * dh, (i + 1) * dh)
            state = state_ref[i]
            r = _dot(lhs_ref[i, c], state.astype(BF))
            o = r[dh:] + o0_ref[i, pl.ds(r0, CHUNK), :]
            tail = cols_ref[pl.ds(r0 + CHUNK - SUBLANES, SUBLANES), :]
            g_last = tail[SUBLANES - 1:SUBLANES, SUBLANES + i:SUBLANES + i + 1]
            state_ref[i] = state * jnp.exp(g_last) - r[:dh] + n_ref[i, c]
            on = o * lax.rsqrt(jnp.mean(o * o, axis=-1, keepdims=True) + EPS) * nw
            y_ref[pl.ds(r0, CHUNK), lanes] = (on * _silu(z_ref[pl.ds(r0, CHUNK), lanes])).astype(y_ref.dtype)


def _delta_group(proj, ba_t, conv_w, a_log, dt_bias, nw, batch, seq, cc, heads, dh):
    sc = conv_w.shape[0]
    dn = heads * dh
    hg = _pick(heads, 8)
    width = hg * dh
    assert (2 * cc) % width == 0 and hg <= SUBLANES
    ts = _pick(seq, 256)
    ns = seq // ts
    g = ts // CHUNK
    t = batch * seq
    nba = ba_t.shape[0]
    hrows = _round_up(sc - 1, SUBLANES)
    tok = lambda col0: pl.BlockSpec((ts, width), lambda b, h, s: (b * ns + s, col0 // width + h))
    cw = lambda col0: pl.BlockSpec((sc, width), lambda b, h, s: (0, col0 // width + h))
    smem = pl.BlockSpec(memory_space=pltpu.SMEM)
    return pl.pallas_call(
        functools.partial(_delta_kernel, heads=heads, hg=hg, sc=sc),
        grid=(batch, heads // hg, ns),
        in_specs=[
            smem, smem,
            tok(2 * cc), tok(2 * cc + dn), tok(2 * cc + 2 * dn), tok(2 * cc + 3 * dn),
            pl.BlockSpec((nba, ts), lambda b, h, s: (0, b * ns + s)),
            cw(0), cw(dn), cw(2 * dn),
            pl.BlockSpec((1, dh), lambda b, h, s: (0, 0)),
        ],
        out_specs=pl.BlockSpec((ts, width), lambda b, h, s: (b * ns + s, h)),
        out_shape=jax.ShapeDtypeStruct((t, dn), BF),
        scratch_shapes=[
            pltpu.VMEM((hg, dh, dh), F32),
            pltpu.VMEM((3, hrows, width), F32),
            pltpu.VMEM((hrows + ts, dh), F32),
            pltpu.VMEM((ts, LANES), F32),
            pltpu.VMEM((hg, CHUNK, ts), F32),
            pltpu.VMEM((hg, ts, ts), BF),
            pltpu.VMEM((hg, ts, 2 * dh), F32),
            pltpu.VMEM((hg, ts, dh), F32),
            pltpu.VMEM((hg, dh, ts), F32),
            pltpu.VMEM((hg, g, dh + CHUNK, dh), BF),
            pltpu.VMEM((hg, g, dh, dh), F32),
            pltpu.VMEM((hg, ts, dh), F32),
        ],
        compiler_params=_params(("arbitrary", "arbitrary", "arbitrary")),
        name="delta_group",
    )(a_log, dt_bias, proj, proj, proj, proj, ba_t, conv_w, conv_w, conv_w, nw.reshape(1, dh))


def _outproj_kernel(x_ref, yc_ref, yd_ref, wc_ref, wd_ref, gate_ref, o_ref):
    y = (jnp.dot(yc_ref[...], wc_ref[...], preferred_element_type=F32)
         + jnp.dot(yd_ref[...], wd_ref[...], preferred_element_type=F32))
    o_ref[...] = x_ref[...] + gate_ref[...] * y


def _outproj(x2, y_conv, y_dn, w_out16, layer, mod4, seq):
    t, d = x2.shape
    cc = y_conv.shape[1]
    dn = y_dn.shape[1]
    assert cc % dn == 0
    tm = _pick(seq, 512)
    per_b = seq // tm
    return pl.pallas_call(
        _outproj_kernel,
        grid=(t // tm,),
        in_specs=[
            pl.BlockSpec((tm, d), lambda i: (i, 0)),
            pl.BlockSpec((tm, cc), lambda i: (i, 0)),
            pl.BlockSpec((tm, dn), lambda i: (i, 0)),
            pl.BlockSpec((None, cc, d), lambda i: (layer, 0, 0)),
            pl.BlockSpec((None, dn, d), lambda i: (layer, cc // dn, 0)),
            pl.BlockSpec((None, 1, d), lambda i: (i // per_b, 0, 2)),
        ],
        out_specs=pl.BlockSpec((tm, d), lambda i: (i, 0)),
        out_shape=jax.ShapeDtypeStruct((t, d), F32),
        compiler_params=_params(("parallel",)),
        name="outproj",
    )(x2, y_conv, y_dn, w_out16, w_out16, mod4)


def _mlp_kernel(x_ref, shift_ref, scale_ref, gate_ref, nw_ref, wu_ref, wd_ref, o_ref, h_ref):
    j = pl.program_id(1)

    @pl.when(j == 0)
    def _():
        _norm_mod_rows(x_ref, nw_ref, shift_ref, scale_ref, h_ref, 16)
        o_ref[...] = jnp.zeros_like(o_ref)

    a = jnp.maximum(jnp.dot(h_ref[...], wu_ref[...], preferred_element_type=F32), 0.0)
    o_ref[...] += jnp.dot((a * a).astype(BF), wd_ref[...], preferred_element_type=F32)

    @pl.when(j == pl.num_programs(1) - 1)
    def _():
        o_ref[...] = x_ref[...] + gate_ref[...] * o_ref[...]


def _mlp(x2, mod4, nw, w_up16, w_down16, layer, seq):
    t, d = x2.shape
    f = w_up16.shape[-1]
    tm = _pick(seq, 1024)
    tf = _pick(f, 512)
    per_b = seq // tm
    modspec = lambda which: pl.BlockSpec((None, 1, d), lambda i, j: (i // per_b, 0, which))
    return pl.pallas_call(
        _mlp_kernel,
        grid=(t // tm, f // tf),
        in_specs=[
            pl.BlockSpec((tm, d), lambda i, j: (i, 0)),
            modspec(3), modspec(4), modspec(5),
            pl.BlockSpec((1, d), lambda i, j: (0, 0)),
            pl.BlockSpec((None, d, tf), lambda i, j: (layer, 0, j)),
            pl.BlockSpec((None, tf, d), lambda i, j: (layer, j, 0)),
        ],
        out_specs=pl.BlockSpec((tm, d), lambda i, j: (i, 0)),
        out_shape=jax.ShapeDtypeStruct((t, d), F32),
        scratch_shapes=[pltpu.VMEM((tm, d), BF)],
        compiler_params=_params(("parallel", "arbitrary")),
        name="mlp",
    )(x2, mod4, mod4, mod4, nw, w_up16, w_down16)


def _final_kernel(x_ref, shift_ref, scale_ref, nw_ref, o_ref):
    _norm_mod_rows(x_ref, nw_ref, shift_ref, scale_ref, o_ref, 16)


def _final(x2, modf, nw, seq):
    t, d = x2.shape
    tm = _pick(seq, 512)
    per_b = seq // tm
    return pl.pallas_call(
        _final_kernel,
        grid=(t // tm,),
        in_specs=[
            pl.BlockSpec((tm, d), lambda i: (i, 0)),
            pl.BlockSpec((None, 1, d), lambda i: (i // per_b, 0, 0)),
            pl.BlockSpec((None, 1, d), lambda i: (i // per_b, 0, 1)),
            pl.BlockSpec((1, d), lambda i: (0, 0)),
        ],
        out_specs=pl.BlockSpec((tm, d), lambda i: (i, 0)),
        out_shape=jax.ShapeDtypeStruct((t, d), F32),
        compiler_params=_params(("parallel",)),
        name="final_norm",
    )(x2, modf, modf, nw)


def kernel(x, c, w_ada, b_ada, norm1_w, w_in, conv_dw_w, conv_dw_b, conv_ln_w, conv_ln_b, w_pw2, conv_out_norm_w, qkv_conv_w, a_log, dt_bias, dn_norm_w, w_out, norm2_w, w_up, w_down, final_ada_w, final_ada_b, final_norm_w):
    batch, seq, d = x.shape
    depth = w_ada.shape[0]
    cc = conv_dw_w.shape[-1]
    heads = a_log.shape[-1]
    dh = dn_norm_w.shape[-1]
    dn = heads * dh
    n_main = 2 * cc + 4 * dn
    assert seq % CHUNK == 0 and dh % LANES == 0 and cc % LANES == 0
    assert w_in.shape[-1] == n_main + 2 * heads

    mods = _ada(c, w_ada, b_ada)
    modf = _ada(c, final_ada_w[None], final_ada_b[None])[0]
    modf = modf.reshape(batch, 1, 2 * d)

    w_in16, w_pw16, w_out16 = w_in.astype(BF), w_pw2.astype(BF), w_out.astype(BF)
    w_up16, w_down16 = w_up.astype(BF), w_down.astype(BF)
    w_ba_t = jnp.swapaxes(w_in[:, :, n_main:], 1, 2).astype(BF)

    x2 = x.reshape(batch * seq, d)
    for l in range(depth):
        mod4 = mods[l].reshape(batch, 1, 6 * d)
        proj, ba_t = _inproj(x2, mod4, norm1_w[l].reshape(1, d), w_in16, w_ba_t[l], l, n_main, seq)
        y_conv = _conv_group(proj, conv_dw_w[l], conv_dw_b[l], conv_ln_w[l], conv_ln_b[l],
                             w_pw16, l, conv_out_norm_w[l], batch, seq)
        y_dn = _delta_group(proj, ba_t, qkv_conv_w[l], a_log[l], dt_bias[l], dn_norm_w[l],
                            batch, seq, cc, heads, dh)
        x2 = _outproj(x2, y_conv, y_dn, w_out16, l, mod4, seq)
        x2 = _mlp(x2, mod4, norm2_w[l].reshape(1, d), w_up16, w_down16, l, seq)
    out = _final(x2, modf, final_norm_w.reshape(1, d), seq)
    return out.reshape(batch, seq, d)
```

```python
import functools

import jax
import jax.numpy as jnp
from jax import lax
from jax.experimental import pallas as pl
from jax.experimental.pallas import tpu as pltpu

EPS = 1e-6
CHUNK = 64
SUBLANES = 8
LANES = 128
VMEM_LIMIT = 56 * 1024 * 1024
CAST_BLOCK_BYTES = 8 * 1024 * 1024
BF = jnp.bfloat16
F32 = jnp.float32


def _round_up(n, m):
    return (n + m - 1) // m * m


def _pick(n, pref):
    t = min(n, pref)
    while n % t:
        t //= 2
    return t


def _params(sem):
    return pltpu.CompilerParams(dimension_semantics=sem, vmem_limit_bytes=VMEM_LIMIT)


def _sigmoid(v):
    return jax.nn.sigmoid(v)


def _silu(v):
    return v * jax.nn.sigmoid(v)


def _cast_kernel(w_ref, o_ref):
    o_ref[...] = w_ref[:, :o_ref.shape[1]].astype(o_ref.dtype)


def _to_bf16(w, n_keep=None):
    nl, k, n = w.shape
    n_keep = n if n_keep is None else n_keep
    rows = max(2 * SUBLANES, CAST_BLOCK_BYTES // (4 * n))
    tk = _pick(k, 1 << (rows.bit_length() - 1))
    return pl.pallas_call(
        _cast_kernel,
        grid=(nl, k // tk),
        in_specs=[pl.BlockSpec((None, tk, n), lambda l, i: (l, i, 0))],
        out_specs=pl.BlockSpec((None, tk, n_keep), lambda l, i: (l, i, 0)),
        out_shape=jax.ShapeDtypeStruct((nl, k, n_keep), BF),
        compiler_params=_params(("parallel", "parallel")),
        name="to_bf16",
    )(w)


def _ada_kernel(c_ref, w_ref, b_ref, o_ref):
    ca = _silu(c_ref[...]).astype(BF)
    o_ref[...] = jnp.dot(ca, w_ref[...].astype(BF), preferred_element_type=F32) + b_ref[...]


def _ada(c, w, b):
    nl, d, n = w.shape
    nb = c.shape[0]
    tn = _pick(n, 1024)
    return pl.pallas_call(
        _ada_kernel,
        grid=(nl, n // tn),
        in_specs=[
            pl.BlockSpec((nb, d), lambda l, j: (0, 0)),
            pl.BlockSpec((None, d, tn), lambda l, j: (l, 0, j)),
            pl.BlockSpec((None, 1, tn), lambda l, j: (l, 0, j)),
        ],
        out_specs=pl.BlockSpec((None, nb, tn), lambda l, j: (l, 0, j)),
        out_shape=jax.ShapeDtypeStruct((nl, nb, n), F32),
        compiler_params=_params(("parallel", "parallel")),
        name="ada",
    )(c, w, b.reshape(nl, 1, n))


def _norm_mod_rows(x_ref, nw_ref, shift_ref, scale_ref, h_ref, rows_per_step):
    tm = x_ref.shape[0]
    gain = nw_ref[...] * (1.0 + scale_ref[...])
    shift = shift_ref[...]
    for r in range(tm // rows_per_step):
        rows = pl.ds(r * rows_per_step, rows_per_step)
        xr = x_ref[rows, :]
        inv = lax.rsqrt(jnp.mean(xr * xr, axis=-1, keepdims=True) + EPS)
        h_ref[rows, :] = (x_ref[rows, :] * inv * gain + shift).astype(h_ref.dtype)


def _inproj_kernel(x_ref, shift_ref, scale_ref, nw_ref, w_ref, wba_ref, proj_ref, ba_ref, h_ref):
    @pl.when(pl.program_id(1) == 0)
    def _():
        _norm_mod_rows(x_ref, nw_ref, shift_ref, scale_ref, h_ref, 16)
        ba_ref[...] = lax.dot_general(wba_ref[...], h_ref[...], (((1,), (1,)), ((), ())),
                                      preferred_element_type=F32)

    proj_ref[...] = jnp.dot(h_ref[...], w_ref[...], preferred_element_type=F32)


def _inproj(x2, mod4, nw, w_in16, w_ba_t, layer, n, seq):
    t, d = x2.shape
    nba = w_ba_t.shape[0]
    tm = _pick(seq, 1024)
    tn = _pick(n, 1024)
    per_b = seq // tm
    return pl.pallas_call(
        _inproj_kernel,
        grid=(t // tm, n // tn),
        in_specs=[
            pl.BlockSpec((tm, d), lambda i, j: (i, 0)),
            pl.BlockSpec((None, 1, d), lambda i, j: (i // per_b, 0, 0)),
            pl.BlockSpec((None, 1, d), lambda i, j: (i // per_b, 0, 1)),
            pl.BlockSpec((1, d), lambda i, j: (0, 0)),
            pl.BlockSpec((None, d, tn), lambda i, j: (layer, 0, j)),
            pl.BlockSpec((nba, d), lambda i, j: (0, 0)),
        ],
        out_specs=[
            pl.BlockSpec((tm, tn), lambda i, j: (i, j)),
            pl.BlockSpec((nba, tm), lambda i, j: (0, i)),
        ],
        out_shape=[
            jax.ShapeDtypeStruct((t, n), F32),
            jax.ShapeDtypeStruct((nba, t), F32),
        ],
        scratch_shapes=[pltpu.VMEM((tm, d), BF)],
        compiler_params=_params(("parallel", "arbitrary")),
        name="inproj",
    )(x2, mod4, mod4, nw, w_in16, w_ba_t)


def _conv_kernel(val_ref, gate_ref, dww_ref, dwb_ref, lnw_ref, lnb_ref, pw_ref, onw_ref,
                 y_ref, ubuf_ref, cbuf_ref, abuf_ref, *, kw, halo, rb, cr):
    ts, cc = val_ref.shape
    ncol = cc // LANES

    @pl.when(pl.program_id(1) == 0)
    def _():
        ubuf_ref[:, pl.ds(0, halo), :] = jnp.zeros((ncol, halo, LANES), F32)

    def glu(r, carry):
        r0 = pl.multiple_of(r * rb, rb)
        for cb in range(ncol):
            lanes = slice(cb * LANES, (cb + 1) * LANES)
            ubuf_ref[cb, pl.ds(halo + r0, rb), :] = (
                val_ref[pl.ds(r0, rb), lanes] * _sigmoid(gate_ref[pl.ds(r0, rb), lanes]))
        return carry

    lax.fori_loop(0, ts // rb, glu, 0)

    lnw = lnw_ref[...]
    lnb = lnb_ref[...]

    first = halo - (kw - 1)
    for cb in range(ncol):
        lanes = slice(cb * LANES, (cb + 1) * LANES)
        for r0 in range(0, ts, cr):
            acc = jnp.broadcast_to(dwb_ref[:, lanes], (cr, LANES))
            for k in range(kw):
                acc = acc + ubuf_ref[cb, pl.ds(r0 + first + k, cr), :] * dww_ref[pl.ds(k, 1), lanes]
            cbuf_ref[pl.ds(r0, cr), lanes] = acc

    for r in range(ts // rb):
        rows = pl.ds(r * rb, rb)
        acc = cbuf_ref[rows, :]
        mu = jnp.mean(acc, axis=-1, keepdims=True)
        xc = acc - mu
        var = jnp.mean(xc * xc, axis=-1, keepdims=True)
        yn = xc * lax.rsqrt(var + EPS) * lnw + lnb
        abuf_ref[rows, :] = _silu(yn).astype(BF)

    ubuf_ref[:, pl.ds(0, halo), :] = ubuf_ref[:, pl.ds(ts, halo), :]

    p = jnp.dot(abuf_ref[...], pw_ref[...], preferred_element_type=F32)
    inv = lax.rsqrt(jnp.mean(p * p, axis=-1, keepdims=True) + EPS)
    y_ref[...] = (p * inv * onw_ref[...]).astype(y_ref.dtype)


def _conv_group(proj, dww, dwb, lnw, lnb, pw16, layer, onw, batch, seq):
    kw, cc = dww.shape
    halo = _round_up(kw - 1, SUBLANES)
    ts = _pick(seq, 256)
    ns = seq // ts
    t = batch * seq
    row = lambda a: a.reshape(1, cc)
    return pl.pallas_call(
        functools.partial(_conv_kernel, kw=kw, halo=halo, rb=16, cr=_pick(ts, 64)),
        grid=(batch, ns),
        in_specs=[
            pl.BlockSpec((ts, cc), lambda b, s: (b * ns + s, 0)),
            pl.BlockSpec((ts, cc), lambda b, s: (b * ns + s, 1)),
            pl.BlockSpec((kw, cc), lambda b, s: (0, 0)),
            pl.BlockSpec((1, cc), lambda b, s: (0, 0)),
            pl.BlockSpec((1, cc), lambda b, s: (0, 0)),
            pl.BlockSpec((1, cc), lambda b, s: (0, 0)),
            pl.BlockSpec((None, cc, cc), lambda b, s: (layer, 0, 0)),
            pl.BlockSpec((1, cc), lambda b, s: (0, 0)),
        ],
        out_specs=pl.BlockSpec((ts, cc), lambda b, s: (b * ns + s, 0)),
        out_shape=jax.ShapeDtypeStruct((t, cc), BF),
        scratch_shapes=[
            pltpu.VMEM((cc // LANES, halo + ts, LANES), F32),
            pltpu.VMEM((ts, cc), F32),
            pltpu.VMEM((ts, cc), BF),
        ],
        compiler_params=_params(("arbitrary", "arbitrary")),
        name="conv_group",
    )(proj, proj, dww, row(dwb), row(lnw), row(lnb), pw16, row(onw))


def _dot_nt(a, b):
    return lax.dot_general(a, b, (((1,), (1,)), ((), ())), preferred_element_type=F32)


def _dot(a, b):
    return jnp.dot(a, b, preferred_element_type=F32)


def _blockdiag(xp, same_chunk):
    g = xp.shape[1] // CHUNK
    return jnp.where(same_chunk, jnp.concatenate([xp] * g, axis=0), 0.0).astype(BF)


def _packed_mm(xs, y, same_chunk):
    a = _dot(jnp.concatenate(list(xs), axis=0).astype(BF), _blockdiag(y, same_chunk))
    return [a[i * CHUNK:(i + 1) * CHUNK] for i in range(len(xs))]


def _packed_unit_lower_inverse(lps, same_chunk):
    ts = lps[0].shape[1]
    pi = lax.broadcasted_iota(jnp.int32, (CHUNK, ts), 0)
    pj = lax.broadcasted_iota(jnp.int32, (CHUNK, ts), 1) % CHUNK
    eye = (pi == pj).astype(F32)
    same16 = (pi // 16) == (pj // 16)
    same32 = (pi // 32) == (pj // 32)
    mm = lambda xs_list, ys: [_packed_mm(xs, y, same_chunk) for xs, y in zip(xs_list, ys)]

    l16 = [jnp.where(same16, lp, 0.0) for lp in lps]
    l2 = [r[0] for r in mm([[a] for a in l16], l16)]
    x = [eye - a for a in l16]
    r = mm([[a, b] for a, b in zip(x, l2)], l2)
    x = [a + b[0] for a, b in zip(x, r)]
    l4 = [b[1] for b in r]
    r = mm([[a, b] for a, b in zip(x, l4)], l4)
    x = [a + b[0] for a, b in zip(x, r)]
    l8 = [b[1] for b in r]
    r = mm([[a] for a in x], l8)
    x = [a + b[0] for a, b in zip(x, r)]
    for keep in (same32 & jnp.logical_not(same16), jnp.logical_not(same32)):
        cm = [jnp.where(keep, lp, 0.0) for lp in lps]
        t1 = [b[0] for b in mm([[a] for a in x], cm)]
        t2 = [b[0] for b in mm([[a] for a in t1], x)]
        x = [a - b for a, b in zip(x, t2)]
    return x


def _delta_kernel(alog_ref, dtb_ref, q_ref, k_ref, v_ref, z_ref, ba_ref, wq_ref, wk_ref, wv_ref,
                  nw_ref, y_ref, state_ref, halo_ref, ext_ref, cols_ref, lp_ref, attn_ref, rhs_ref,
                  qd_ref, kdt_ref, lhs_ref, n_ref, o0_ref, *, heads, hg, sc):
    ts, width = q_ref.shape
    dh = width // hg
    g = ts // CHUNK
    head0 = pl.program_id(1) * hg
    hrows = halo_ref.shape[1]

    @pl.when(pl.program_id(2) == 0)
    def _():
        state_ref[...] = jnp.zeros_like(state_ref)
        halo_ref[...] = jnp.zeros_like(halo_ref)

    pad = [jnp.zeros((SUBLANES - hg, ts), F32)] if hg < SUBLANES else []
    betas, gs = [], []
    for i in range(hg):
        b_logit = ba_ref[pl.ds(head0 + i, 1), :]
        xs = ba_ref[pl.ds(heads + head0 + i, 1), :] + dtb_ref[head0 + i]
        softplus = jnp.maximum(xs, 0.0) + jnp.log(1.0 + jnp.exp(-jnp.abs(xs)))
        betas.append(_sigmoid(b_logit))
        gs.append(-jnp.exp(jnp.full((1, ts), alog_ref[head0 + i], F32)) * softplus)
    beta_rows = jnp.concatenate(betas + pad, axis=0)
    g_rows = jnp.concatenate(gs + pad, axis=0)
    lane = lax.broadcasted_iota(jnp.int32, (SUBLANES, ts), 1) % CHUNK
    d = 1
    while d < CHUNK:
        g_rows = g_rows + jnp.where(lane >= d, pltpu.roll(g_rows, d, 1), 0.0)
        d *= 2
    gl_rows = jnp.where(lane == CHUNK - 1, g_rows, 0.0)
    d = 1
    while d < CHUNK:
        gl_rows = gl_rows + jnp.where(lane + d < CHUNK, pltpu.roll(gl_rows, ts - d, 1), 0.0)
        d *= 2
    kdec_rows = jnp.exp(gl_rows - g_rows)
    rows = jnp.concatenate([beta_rows, g_rows, jnp.zeros((LANES - 2 * SUBLANES, ts), F32)], axis=0)
    cols_ref[...] = rows.T

    ri = lax.broadcasted_iota(jnp.int32, (ts, ts), 0)
    ci = lax.broadcasted_iota(jnp.int32, (ts, ts), 1)
    same_chunk = (ri // CHUNK) == (ci // CHUNK)
    causal = same_chunk & (ri >= ci)

    def short_conv(idx, x_ref, w_ref, lanes):
        ext_ref[pl.ds(0, hrows), :] = halo_ref[idx, :, lanes]
        ext_ref[pl.ds(hrows, ts), :] = x_ref[:, lanes]
        acc = jnp.zeros((ts, dh), F32)
        for j in range(sc):
            acc = acc + ext_ref[pl.ds(hrows - (sc - 1) + j, ts), :] * w_ref[pl.ds(j, 1), lanes]
        halo_ref[idx, :, lanes] = x_ref[pl.ds(ts - hrows, hrows), lanes]
        return _silu(acc)

    for i in range(hg):
        lanes = slice(i * dh, (i + 1) * dh)
        q = short_conv(0, q_ref, wq_ref, lanes)
        k = short_conv(1, k_ref, wk_ref, lanes)
        v = short_conv(2, v_ref, wv_ref, lanes)
        q = q * lax.rsqrt(jnp.sum(q * q, axis=-1, keepdims=True) + EPS) * (dh ** -0.5)
        k = k * lax.rsqrt(jnp.sum(k * k, axis=-1, keepdims=True) + EPS)
        beta_col = cols_ref[:, i:i + 1]
        g_col = cols_ref[:, SUBLANES + i:SUBLANES + i + 1]
        g_row = g_rows[i:i + 1]
        eg = jnp.exp(g_col)
        kb = k * beta_col
        rhs_ref[i] = jnp.concatenate([v * beta_col, kb * eg], axis=1)
        qd_ref[i] = q * eg
        decay = jnp.exp(jnp.where(causal, g_col - g_row, -jnp.inf))
        kq = _dot_nt(jnp.concatenate([kb, q], axis=0).astype(BF), k.astype(BF))
        attn_ref[i] = (kq[ts:] * decay).astype(BF)
        lfull = jnp.where(ri > ci, kq[:ts] * decay, 0.0)
        lp = lfull[0:CHUNK]
        for c in range(1, g):
            lp = lp + lfull[c * CHUNK:(c + 1) * CHUNK]
        lp_ref[i] = lp
        kdt_ref[i] = k.T * kdec_rows[i:i + 1]

    tinv = _packed_unit_lower_inverse([lp_ref[i] for i in range(hg)], same_chunk)

    cj = lax.broadcasted_iota(jnp.int32, (dh, ts), 1) // CHUNK
    eye_p = (lax.broadcasted_iota(jnp.int32, (CHUNK, ts), 0)
             == lax.broadcasted_iota(jnp.int32, (CHUNK, ts), 1) % CHUNK).astype(F32)
    for i in range(hg):
        t_off = _blockdiag(tinv[i] - eye_p, same_chunk)
        rhs = rhs_ref[i]
        uw = rhs + _dot(t_off, rhs.astype(BF))
        uw16 = uw.astype(BF)
        auw = _dot(attn_ref[i], uw16)
        o0_ref[i] = auw[:, :dh]
        qp16 = (qd_ref[i] - auw[:, dh:]).astype(BF)
        kdt = kdt_ref[i]
        for c in range(g):
            kun = _dot(jnp.where(cj == c, kdt, 0.0).astype(BF), uw16)
            n_ref[i, c] = kun[:, :dh]
            lhs_ref[i, c] = jnp.concatenate(
                [kun[:, dh:].astype(BF), qp16[c * CHUNK:(c + 1) * CHUNK]], axis=0)

    nw = nw_ref[...]
    for c in range(g):
        r0 = c * CHUNK
        for i in range(hg):
            lanes = slice(i * dh, (i + 1) * dh)
            state = state_ref[i]
            r = _dot(lhs_ref[i, c], state.astype(BF))
            o = r[dh:] + o0_ref[i, pl.ds(r0, CHUNK), :]
            tail = cols_ref[pl.ds(r0 + CHUNK - SUBLANES, SUBLANES), :]
            g_last = tail[SUBLANES - 1:SUBLANES, SUBLANES + i:SUBLANES + i + 1]
            state_ref[i] = state * jnp.exp(g_last) - r[:dh] + n_ref[i, c]
            on = o * lax.rsqrt(jnp.mean(o * o, axis=-1, keepdims=True) + EPS) * nw
            y_ref[pl.ds(r0, CHUNK), lanes] = (on * _silu(z_ref[pl.ds(r0, CHUNK), lanes])).astype(y_ref.dtype)


def _delta_group(proj, ba_t, conv_w, a_log, dt_bias, nw, batch, seq, cc, heads, dh):
    sc = conv_w.shape[0]
    dn = heads * dh
    hg = _pick(heads, 8)
    width = hg * dh
    assert (2 * cc) % width == 0 and hg <= SUBLANES
    ts = _pick(seq, 256)
    ns = seq // ts
    g = ts // CHUNK
    t = batch * seq
    nba = ba_t.shape[0]
    hrows = _round_up(sc - 1, SUBLANES)
    tok = lambda col0: pl.BlockSpec((ts, width), lambda b, h, s: (b * ns + s, col0 // width + h))
    cw = lambda col0: pl.BlockSpec((sc, width), lambda b, h, s: (0, col0 // width + h))
    smem = pl.BlockSpec(memory_space=pltpu.SMEM)
    return pl.pallas_call(
        functools.partial(_delta_kernel, heads=heads, hg=hg, sc=sc),
        grid=(batch, heads // hg, ns),
        in_specs=[
            smem, smem,
            tok(2 * cc), tok(2 * cc + dn), tok(2 * cc + 2 * dn), tok(2 * cc + 3 * dn),
            pl.BlockSpec((nba, ts), lambda b, h, s: (0, b * ns + s)),
            cw(0), cw(dn), cw(2 * dn),
            pl.BlockSpec((1, dh), lambda b, h, s: (0, 0)),
        ],
        out_specs=pl.BlockSpec((ts, width), lambda b, h, s: (b * ns + s, h)),
        out_shape=jax.ShapeDtypeStruct((t, dn), BF),
        scratch_shapes=[
            pltpu.VMEM((hg, dh, dh), F32),
            pltpu.VMEM((3, hrows, width), F32),
            pltpu.VMEM((hrows + ts, dh), F32),
            pltpu.VMEM((ts, LANES), F32),
            pltpu.VMEM((hg, CHUNK, ts), F32),
            pltpu.VMEM((hg, ts, ts), BF),
            pltpu.VMEM((hg, ts, 2 * dh), F32),
            pltpu.VMEM((hg, ts, dh), F32),
            pltpu.VMEM((hg, dh, ts), F32),
            pltpu.VMEM((hg, g, dh + CHUNK, dh), BF),
            pltpu.VMEM((hg, g, dh, dh), F32),
            pltpu.VMEM((hg, ts, dh), F32),
        ],
        compiler_params=_params(("arbitrary", "arbitrary", "arbitrary")),
        name="delta_group",
    )(a_log, dt_bias, proj, proj, proj, proj, ba_t, conv_w, conv_w, conv_w, nw.reshape(1, dh))


def _outproj_kernel(x_ref, yc_ref, yd_ref, wc_ref, wd_ref, gate_ref, o_ref):
    y = (jnp.dot(yc_ref[...], wc_ref[...], preferred_element_type=F32)
         + jnp.dot(yd_ref[...], wd_ref[...], preferred_element_type=F32))
    o_ref[...] = x_ref[...] + gate_ref[...] * y


def _outproj(x2, y_conv, y_dn, w_out16, layer, mod4, seq):
    t, d = x2.shape
    cc = y_conv.shape[1]
    dn = y_dn.shape[1]
    assert cc % dn == 0
    tm = _pick(seq, 512)
    per_b = seq // tm
    return pl.pallas_call(
        _outproj_kernel,
        grid=(t // tm,),
        in_specs=[
            pl.BlockSpec((tm, d), lambda i: (i, 0)),
            pl.BlockSpec((tm, cc), lambda i: (i, 0)),
            pl.BlockSpec((tm, dn), lambda i: (i, 0)),
            pl.BlockSpec((None, cc, d), lambda i: (layer, 0, 0)),
            pl.BlockSpec((None, dn, d), lambda i: (layer, cc // dn, 0)),
            pl.BlockSpec((None, 1, d), lambda i: (i // per_b, 0, 2)),
        ],
        out_specs=pl.BlockSpec((tm, d), lambda i: (i, 0)),
        out_shape=jax.ShapeDtypeStruct((t, d), F32),
        compiler_params=_params(("parallel",)),
        name="outproj",
    )(x2, y_conv, y_dn, w_out16, w_out16, mod4)


def _mlp_kernel(x_ref, shift_ref, scale_ref, gate_ref, nw_ref, wu_ref, wd_ref,
                fshift_ref, fscale_ref, fnw_ref, o_ref, h_ref, *, final):
    j = pl.program_id(1)

    @pl.when(j == 0)
    def _():
        _norm_mod_rows(x_ref, nw_ref, shift_ref, scale_ref, h_ref, 16)
        o_ref[...] = jnp.zeros_like(o_ref)

    a = jnp.maximum(jnp.dot(h_ref[...], wu_ref[...], preferred_element_type=F32), 0.0)
    o_ref[...] += jnp.dot((a * a).astype(BF), wd_ref[...], preferred_element_type=F32)

    @pl.when(j == pl.num_programs(1) - 1)
    def _():
        o_ref[...] = x_ref[...] + gate_ref[...] * o_ref[...]
        if final:
            _norm_mod_rows(o_ref, fnw_ref, fshift_ref, fscale_ref, o_ref, 16)


def _mlp(x2, mod4, nw, w_up16, w_down16, layer, modf, fnw, final, seq):
    t, d = x2.shape
    f = w_up16.shape[-1]
    tm = _pick(seq, 1024)
    tf = _pick(f, 512)
    per_b = seq // tm
    modspec = lambda which: pl.BlockSpec((None, 1, d), lambda i, j: (i // per_b, 0, which))
    return pl.pallas_call(
        functools.partial(_mlp_kernel, final=final),
        grid=(t // tm, f // tf),
        in_specs=[
            pl.BlockSpec((tm, d), lambda i, j: (i, 0)),
            modspec(3), modspec(4), modspec(5),
            pl.BlockSpec((1, d), lambda i, j: (0, 0)),
            pl.BlockSpec((None, d, tf), lambda i, j: (layer, 0, j)),
            pl.BlockSpec((None, tf, d), lambda i, j: (layer, j, 0)),
            pl.BlockSpec((None, 1, d), lambda i, j: (i // per_b, 0, 0)),
            pl.BlockSpec((None, 1, d), lambda i, j: (i // per_b, 0, 1)),
            pl.BlockSpec((1, d), lambda i, j: (0, 0)),
        ],
        out_specs=pl.BlockSpec((tm, d), lambda i, j: (i, 0)),
        out_shape=jax.ShapeDtypeStruct((t, d), F32),
        scratch_shapes=[pltpu.VMEM((tm, d), BF)],
        compiler_params=_params(("parallel", "arbitrary")),
        name="mlp",
    )(x2, mod4, mod4, mod4, nw, w_up16, w_down16, modf, modf, fnw)


def kernel(x, c, w_ada, b_ada, norm1_w, w_in, conv_dw_w, conv_dw_b, conv_ln_w, conv_ln_b, w_pw2, conv_out_norm_w, qkv_conv_w, a_log, dt_bias, dn_norm_w, w_out, norm2_w, w_up, w_down, final_ada_w, final_ada_b, final_norm_w):
    batch, seq, d = x.shape
    depth = w_ada.shape[0]
    cc = conv_dw_w.shape[-1]
    heads = a_log.shape[-1]
    dh = dn_norm_w.shape[-1]
    dn = heads * dh
    n_main = 2 * cc + 4 * dn
    assert seq % CHUNK == 0 and dh % LANES == 0 and cc % LANES == 0
    assert w_in.shape[-1] == n_main + 2 * heads

    mods = _ada(c, w_ada, b_ada)
    modf = _ada(c, final_ada_w[None], final_ada_b[None])[0]
    modf = modf.reshape(batch, 1, 2 * d)

    w_in16, w_pw16, w_out16 = _to_bf16(w_in, n_main), _to_bf16(w_pw2), _to_bf16(w_out)
    w_up16, w_down16 = _to_bf16(w_up), _to_bf16(w_down)
    w_ba_t = jnp.swapaxes(w_in[:, :, n_main:], 1, 2).astype(BF)

    x2 = x.reshape(batch * seq, d)
    for l in range(depth):
        mod4 = mods[l].reshape(batch, 1, 6 * d)
        proj, ba_t = _inproj(x2, mod4, norm1_w[l].reshape(1, d), w_in16, w_ba_t[l], l, n_main, seq)
        y_conv = _conv_group(proj, conv_dw_w[l], conv_dw_b[l], conv_ln_w[l], conv_ln_b[l],
                             w_pw16, l, conv_out_norm_w[l], batch, seq)
        y_dn = _delta_group(proj, ba_t, qkv_conv_w[l], a_log[l], dt_bias[l], dn_norm_w[l],
                            batch, seq, cc, heads, dh)
        x2 = _outproj(x2, y_conv, y_dn, w_out16, l, mod4, seq)
        x2 = _mlp(x2, mod4, norm2_w[l].reshape(1, d), w_up16, w_down16, l,
                  modf, final_norm_w.reshape(1, d), l == depth - 1, seq)
    return x2.reshape(batch, seq, d)
```

```python
import functools

import jax
import jax.numpy as jnp
from jax import lax
from jax.experimental import pallas as pl
from jax.experimental.pallas import tpu as pltpu

EPS = 1e-6
CHUNK = 64
SUBLANES = 8
LANES = 128
VMEM_LIMIT = 56 * 1024 * 1024
CAST_BLOCK_BYTES = 8 * 1024 * 1024
BF = jnp.bfloat16
F32 = jnp.float32


def _round_up(n, m):
    return (n + m - 1) // m * m


def _pick(n, pref):
    t = min(n, pref)
    while n % t:
        t //= 2
    return t


def _params(sem):
    return pltpu.CompilerParams(dimension_semantics=sem, vmem_limit_bytes=VMEM_LIMIT)


def _sigmoid(v):
    return jax.nn.sigmoid(v)


def _silu(v):
    return v * jax.nn.sigmoid(v)


def _cast_kernel(w_ref, o_ref, *rest_refs):
    n_keep = o_ref.shape[1]
    o_ref[...] = w_ref[:, :n_keep].astype(o_ref.dtype)
    for r in rest_refs:
        r[...] = w_ref[:, n_keep:].astype(r.dtype)


def _to_bf16(w, n_keep=None):
    nl, k, n = w.shape
    n_keep = n if n_keep is None else n_keep
    rows = max(2 * SUBLANES, CAST_BLOCK_BYTES // (4 * n))
    tk = _pick(k, 1 << (rows.bit_length() - 1))
    widths = [n_keep] + ([n - n_keep] if n_keep < n else [])
    out = pl.pallas_call(
        _cast_kernel,
        grid=(nl, k // tk),
        in_specs=[pl.BlockSpec((None, tk, n), lambda l, i: (l, i, 0))],
        out_specs=[pl.BlockSpec((None, tk, wd), lambda l, i: (l, i, 0)) for wd in widths],
        out_shape=[jax.ShapeDtypeStruct((nl, k, wd), BF) for wd in widths],
        compiler_params=_params(("parallel", "parallel")),
        name="to_bf16",
    )(w)
    return out if len(out) > 1 else out[0]


def _ada_kernel(c_ref, w_ref, b_ref, o_ref):
    ca = _silu(c_ref[...]).astype(BF)
    o_ref[...] = jnp.dot(ca, w_ref[...].astype(BF), preferred_element_type=F32) + b_ref[...]


def _ada(c, w, b):
    nl, d, n = w.shape
    nb = c.shape[0]
    tn = _pick(n, 1024)
    return pl.pallas_call(
        _ada_kernel,
        grid=(nl, n // tn),
        in_specs=[
            pl.BlockSpec((nb, d), lambda l, j: (0, 0)),
            pl.BlockSpec((None, d, tn), lambda l, j: (l, 0, j)),
            pl.BlockSpec((None, 1, tn), lambda l, j: (l, 0, j)),
        ],
        out_specs=pl.BlockSpec((None, nb, tn), lambda l, j: (l, 0, j)),
        out_shape=jax.ShapeDtypeStruct((nl, nb, n), F32),
        compiler_params=_params(("parallel", "parallel")),
        name="ada",
    )(c, w, b.reshape(nl, 1, n))


def _norm_mod_rows(x_ref, nw_ref, shift_ref, scale_ref, h_ref, rows_per_step):
    tm = x_ref.shape[0]
    gain = nw_ref[...] * (1.0 + scale_ref[...])
    shift = shift_ref[...]
    for r in range(tm // rows_per_step):
        rows = pl.ds(r * rows_per_step, rows_per_step)
        xr = x_ref[rows, :]
        inv = lax.rsqrt(jnp.mean(xr * xr, axis=-1, keepdims=True) + EPS)
        h_ref[rows, :] = (x_ref[rows, :] * inv * gain + shift).astype(h_ref.dtype)


def _inproj_kernel(x_ref, shift_ref, scale_ref, nw_ref, w_ref, wba_ref, proj_ref, ba_ref, h_ref):
    @pl.when(pl.program_id(1) == 0)
    def _():
        _norm_mod_rows(x_ref, nw_ref, shift_ref, scale_ref, h_ref, 16)
        ba_ref[...] = lax.dot_general(wba_ref[...], h_ref[...], (((1,), (1,)), ((), ())),
                                      preferred_element_type=F32)

    proj_ref[...] = jnp.dot(h_ref[...], w_ref[...], preferred_element_type=F32)


def _inproj(x2, mod4, nw, w_in16, w_ba_t, layer, n, seq):
    t, d = x2.shape
    nba = w_ba_t.shape[0]
    tm = _pick(seq, 1024)
    tn = _pick(n, 1024)
    per_b = seq // tm
    return pl.pallas_call(
        _inproj_kernel,
        grid=(t // tm, n // tn),
        in_specs=[
            pl.BlockSpec((tm, d), lambda i, j: (i, 0)),
            pl.BlockSpec((None, 1, d), lambda i, j: (i // per_b, 0, 0)),
            pl.BlockSpec((None, 1, d), lambda i, j: (i // per_b, 0, 1)),
            pl.BlockSpec((1, d), lambda i, j: (0, 0)),
            pl.BlockSpec((None, d, tn), lambda i, j: (layer, 0, j)),
            pl.BlockSpec((nba, d), lambda i, j: (0, 0)),
        ],
        out_specs=[
            pl.BlockSpec((tm, tn), lambda i, j: (i, j)),
            pl.BlockSpec((nba, tm), lambda i, j: (0, i)),
        ],
        out_shape=[
            jax.ShapeDtypeStruct((t, n), F32),
            jax.ShapeDtypeStruct((nba, t), F32),
        ],
        scratch_shapes=[pltpu.VMEM((tm, d), BF)],
        compiler_params=_params(("parallel", "arbitrary")),
        name="inproj",
    )(x2, mod4, mod4, nw, w_in16, w_ba_t)


def _mix_out_kernel(val_ref, gate_ref, dww_ref, dwb_ref, lnw_ref, lnb_ref, pw_ref, onw_ref,
                    ydn_ref, x_ref, wc_ref, wd_ref, g1_ref, o_ref,
                    ubuf_ref, cbuf_ref, abuf_ref, yc_ref, *, kw, halo, rb, cr, per_b):
    ts, cc = val_ref.shape
    ncol = cc // LANES
    s = pl.program_id(0)

    @pl.when(s == 0)
    def _():
        yc_ref[...] = jnp.zeros_like(yc_ref)

    @pl.when(s % per_b == 0)
    def _():
        ubuf_ref[:, pl.ds(0, halo), :] = jnp.zeros((ncol, halo, LANES), F32)

    for cb in range(ncol):
        lanes = slice(cb * LANES, (cb + 1) * LANES)
        for r0 in range(0, ts, cr):
            rows = pl.ds(r0, cr)
            ubuf_ref[cb, pl.ds(halo + r0, cr), :] = val_ref[rows, lanes] * _sigmoid(gate_ref[rows, lanes])

    first = halo - (kw - 1)
    nblk = o_ref.shape[1] // ncol
    for cb in range(ncol):
        ocols = slice(cb * nblk, (cb + 1) * nblk)
        y = (jnp.dot(yc_ref[(s + 1) % 2], wc_ref[:, ocols], preferred_element_type=F32)
             + jnp.dot(ydn_ref[...], wd_ref[:, ocols], preferred_element_type=F32))
        o_ref[:, ocols] = x_ref[:, ocols] + g1_ref[:, ocols] * y

        lanes = slice(cb * LANES, (cb + 1) * LANES)
        for r0 in range(0, ts, cr):
            acc = jnp.broadcast_to(dwb_ref[:, lanes], (cr, LANES))
            for k in range(kw):
                acc = acc + ubuf_ref[cb, pl.ds(r0 + first + k, cr), :] * dww_ref[pl.ds(k, 1), lanes]
            cbuf_ref[pl.ds(r0, cr), lanes] = acc

    lnw = lnw_ref[...]
    lnb = lnb_ref[...]
    for r in range(ts // rb):
        rows = pl.ds(r * rb, rb)
        acc = cbuf_ref[rows, :]
        mu = jnp.mean(acc, axis=-1, keepdims=True)
        xc = acc - mu
        var = jnp.mean(xc * xc, axis=-1, keepdims=True)
        yn = xc * lax.rsqrt(var + EPS) * lnw + lnb
        abuf_ref[rows, :] = _silu(yn).astype(BF)

    ubuf_ref[:, pl.ds(0, halo), :] = ubuf_ref[:, pl.ds(ts, halo), :]

    p = jnp.dot(abuf_ref[...], pw_ref[...], preferred_element_type=F32)
    inv = lax.rsqrt(jnp.mean(p * p, axis=-1, keepdims=True) + EPS)
    yc_ref[s % 2] = (p * inv * onw_ref[...]).astype(BF)


def _mix_out(proj, y_dn, x2, dww, dwb, lnw, lnb, pw16, w_out16, layer, onw, mod4, seq):
    kw, cc = dww.shape
    t, d = x2.shape
    dn = y_dn.shape[1]
    assert cc % dn == 0
    halo = _round_up(kw - 1, SUBLANES)
    ts = _pick(seq, 512)
    per_b = seq // ts
    nt = t // ts
    row = lambda a: a.reshape(1, cc)
    cur = lambda s: jnp.minimum(s, nt - 1)
    prev = lambda s: jnp.maximum(s - 1, 0)
    const = lambda shape: pl.BlockSpec(shape, lambda s: (0,) * len(shape))
    return pl.pallas_call(
        functools.partial(_mix_out_kernel, kw=kw, halo=halo, rb=16, cr=_pick(ts, 64), per_b=per_b),
        grid=(nt + 1,),
        in_specs=[
            pl.BlockSpec((ts, cc), lambda s: (cur(s), 0)),
            pl.BlockSpec((ts, cc), lambda s: (cur(s), 1)),
            const((kw, cc)), const((1, cc)), const((1, cc)), const((1, cc)),
            pl.BlockSpec((None, cc, cc), lambda s: (layer, 0, 0)),
            const((1, cc)),
            pl.BlockSpec((ts, dn), lambda s: (prev(s), 0)),
            pl.BlockSpec((ts, d), lambda s: (prev(s), 0)),
            pl.BlockSpec((None, cc, d), lambda s: (layer, 0, 0)),
            pl.BlockSpec((None, dn, d), lambda s: (layer, cc // dn, 0)),
            pl.BlockSpec((None, 1, d), lambda s: (prev(s) // per_b, 0, 2)),
        ],
        out_specs=pl.BlockSpec((ts, d), lambda s: (prev(s), 0)),
        out_shape=jax.ShapeDtypeStruct((t, d), F32),
        scratch_shapes=[
            pltpu.VMEM((cc // LANES, halo + ts, LANES), F32),
            pltpu.VMEM((ts, cc), F32),
            pltpu.VMEM((ts, cc), BF),
            pltpu.VMEM((2, ts, cc), BF),
        ],
        compiler_params=_params(("arbitrary",)),
        name="mix_out",
    )(proj, proj, dww, row(dwb), row(lnw), row(lnb), pw16, row(onw), y_dn, x2, w_out16, w_out16, mod4)


def _dot_nt(a, b):
    return lax.dot_general(a, b, (((1,), (1,)), ((), ())), preferred_element_type=F32)


def _dot(a, b):
    return jnp.dot(a, b, preferred_element_type=F32)


def _blockdiag(xp, same_chunk):
    g = xp.shape[1] // CHUNK
    return jnp.where(same_chunk, jnp.concatenate([xp] * g, axis=0), 0.0).astype(BF)


def _packed_mm(xs, y, same_chunk):
    a = _dot(jnp.concatenate(list(xs), axis=0).astype(BF), _blockdiag(y, same_chunk))
    return [a[i * CHUNK:(i + 1) * CHUNK] for i in range(len(xs))]


def _packed_unit_lower_inverse(lps, same_chunk):
    ts = lps[0].shape[1]
    pi = lax.broadcasted_iota(jnp.int32, (CHUNK, ts), 0)
    pj = lax.broadcasted_iota(jnp.int32, (CHUNK, ts), 1) % CHUNK
    eye = (pi == pj).astype(F32)
    same16 = (pi // 16) == (pj // 16)
    same32 = (pi // 32) == (pj // 32)
    mm = lambda xs_list, ys: [_packed_mm(xs, y, same_chunk) for xs, y in zip(xs_list, ys)]

    l16 = [jnp.where(same16, lp, 0.0) for lp in lps]
    l2 = [r[0] for r in mm([[a] for a in l16], l16)]
    x = [eye - a for a in l16]
    r = mm([[a, b] for a, b in zip(x, l2)], l2)
    x = [a + b[0] for a, b in zip(x, r)]
    l4 = [b[1] for b in r]
    r = mm([[a, b] for a, b in zip(x, l4)], l4)
    x = [a + b[0] for a, b in zip(x, r)]
    l8 = [b[1] for b in r]
    r = mm([[a] for a in x], l8)
    x = [a + b[0] for a, b in zip(x, r)]
    for keep in (same32 & jnp.logical_not(same16), jnp.logical_not(same32)):
        cm = [jnp.where(keep, lp, 0.0) for lp in lps]
        t1 = [b[0] for b in mm([[a] for a in x], cm)]
        t2 = [b[0] for b in mm([[a] for a in t1], x)]
        x = [a - b for a, b in zip(x, t2)]
    return x


def _delta_kernel(alog_ref, dtb_ref, q_ref, k_ref, v_ref, z_ref, ba_ref, wq_ref, wk_ref, wv_ref,
                  nw_ref, y_ref, state_ref, halo_ref, ext_ref, cols_ref, lp_ref, attn_ref, rhs_ref,
                  qd_ref, kdt_ref, lhs_ref, n_ref, o0_ref, *, heads, hg, sc):
    ts, width = q_ref.shape
    dh = width // hg
    g = ts // CHUNK
    head0 = pl.program_id(1) * hg
    hrows = halo_ref.shape[1]

    @pl.when(pl.program_id(2) == 0)
    def _():
        state_ref[...] = jnp.zeros_like(state_ref)
        halo_ref[...] = jnp.zeros_like(halo_ref)

    pad = [jnp.zeros((SUBLANES - hg, ts), F32)] if hg < SUBLANES else []
    betas, gs = [], []
    for i in range(hg):
        b_logit = ba_ref[pl.ds(head0 + i, 1), :]
        xs = ba_ref[pl.ds(heads + head0 + i, 1), :] + dtb_ref[head0 + i]
        softplus = jnp.maximum(xs, 0.0) + jnp.log(1.0 + jnp.exp(-jnp.abs(xs)))
        betas.append(_sigmoid(b_logit))
        gs.append(-jnp.exp(jnp.full((1, ts), alog_ref[head0 + i], F32)) * softplus)
    beta_rows = jnp.concatenate(betas + pad, axis=0)
    g_rows = jnp.concatenate(gs + pad, axis=0)
    lane = lax.broadcasted_iota(jnp.int32, (SUBLANES, ts), 1) % CHUNK
    d = 1
    while d < CHUNK:
        g_rows = g_rows + jnp.where(lane >= d, pltpu.roll(g_rows, d, 1), 0.0)
        d *= 2
    gl_rows = jnp.where(lane == CHUNK - 1, g_rows, 0.0)
    d = 1
    while d < CHUNK:
        gl_rows = gl_rows + jnp.where(lane + d < CHUNK, pltpu.roll(gl_rows, ts - d, 1), 0.0)
        d *= 2
    kdec_rows = jnp.exp(gl_rows - g_rows)
    rows = jnp.concatenate([beta_rows, g_rows, jnp.zeros((LANES - 2 * SUBLANES, ts), F32)], axis=0)
    cols_ref[...] = rows.T

    ri = lax.broadcasted_iota(jnp.int32, (ts, ts), 0)
    ci = lax.broadcasted_iota(jnp.int32, (ts, ts), 1)
    same_chunk = (ri // CHUNK) == (ci // CHUNK)
    causal = same_chunk & (ri >= ci)

    def short_conv(idx, x_ref, w_ref, lanes):
        ext_ref[pl.ds(0, hrows), :] = halo_ref[idx, :, lanes]
        ext_ref[pl.ds(hrows, ts), :] = x_ref[:, lanes]
        acc = jnp.zeros((ts, dh), F32)
        for j in range(sc):
            acc = acc + ext_ref[pl.ds(hrows - (sc - 1) + j, ts), :] * w_ref[pl.ds(j, 1), lanes]
        halo_ref[idx, :, lanes] = x_ref[pl.ds(ts - hrows, hrows), lanes]
        return _silu(acc)

    for i in range(hg):
        lanes = slice(i * dh, (i + 1) * dh)
        q = short_conv(0, q_ref, wq_ref, lanes)
        k = short_conv(1, k_ref, wk_ref, lanes)
        v = short_conv(2, v_ref, wv_ref, lanes)
        q = q * lax.rsqrt(jnp.sum(q * q, axis=-1, keepdims=True) + EPS) * (dh ** -0.5)
        k = k * lax.rsqrt(jnp.sum(k * k, axis=-1, keepdims=True) + EPS)
        beta_col = cols_ref[:, i:i + 1]
        g_col = cols_ref[:, SUBLANES + i:SUBLANES + i + 1]
        g_row = g_rows[i:i + 1]
        eg = jnp.exp(g_col)
        kb = k * beta_col
        rhs_ref[i] = jnp.concatenate([v * beta_col, kb * eg], axis=1)
        qd_ref[i] = q * eg
        decay = jnp.exp(jnp.where(causal, g_col - g_row, -jnp.inf))
        kq = _dot_nt(jnp.concatenate([kb, q], axis=0).astype(BF), k.astype(BF))
        attn_ref[i] = (kq[ts:] * decay).astype(BF)
        lfull = jnp.where(ri > ci, kq[:ts] * decay, 0.0)
        lp = lfull[0:CHUNK]
        for c in range(1, g):
            lp = lp + lfull[c * CHUNK:(c + 1) * CHUNK]
        lp_ref[i] = lp
        kdt_ref[i] = k.T * kdec_rows[i:i + 1]

    tinv = _packed_unit_lower_inverse([lp_ref[i] for i in range(hg)], same_chunk)

    cj = lax.broadcasted_iota(jnp.int32, (dh, ts), 1) // CHUNK
    eye_p = (lax.broadcasted_iota(jnp.int32, (CHUNK, ts), 0)
             == lax.broadcasted_iota(jnp.int32, (CHUNK, ts), 1) % CHUNK).astype(F32)
    for i in range(hg):
        t_off = _blockdiag(tinv[i] - eye_p, same_chunk)
        rhs = rhs_ref[i]
        uw = rhs + _dot(t_off, rhs.astype(BF))
        uw16 = uw.astype(BF)
        auw = _dot(attn_ref[i], uw16)
        o0_ref[i] = auw[:, :dh]
        qp16 = (qd_ref[i] - auw[:, dh:]).astype(BF)
        kdt = kdt_ref[i]
        for c in range(g):
            kun = _dot(jnp.where(cj == c, kdt, 0.0).astype(BF), uw16)
            n_ref[i, c] = kun[:, :dh]
            lhs_ref[i, c] = jnp.concatenate(
                [kun[:, dh:].astype(BF), qp16[c * CHUNK:(c + 1) * CHUNK]], axis=0)

    nw = nw_ref[...]
    for c in range(g):
        r0 = c * CHUNK
        for i in range(hg):
            lanes = slice(i * dh, (i + 1) * dh)
            state = state_ref[i]
            r = _dot(lhs_ref[i, c], state.astype(BF))
            o = r[dh:] + o0_ref[i, pl.ds(r0, CHUNK), :]
            tail = cols_ref[pl.ds(r0 + CHUNK - SUBLANES, SUBLANES), :]
            g_last = tail[SUBLANES - 1:SUBLANES, SUBLANES + i:SUBLANES + i + 1]
            state_ref[i] = state * jnp.exp(g_last) - r[:dh] + n_ref[i, c]
            on = o * lax.rsqrt(jnp.mean(o * o, axis=-1, keepdims=True) + EPS) * nw
            y_ref[pl.ds(r0, CHUNK), lanes] = (on * _silu(z_ref[pl.ds(r0, CHUNK), lanes])).astype(y_ref.dtype)


def _delta_group(proj, ba_t, conv_w, a_log, dt_bias, nw, batch, seq, cc, heads, dh):
    sc = conv_w.shape[0]
    dn = heads * dh
    hg = _pick(heads, 8)
    width = hg * dh
    assert (2 * cc) % width == 0 and hg <= SUBLANES
    ts = _pick(seq, 256)
    ns = seq // ts
    g = ts // CHUNK
    t = batch * seq
    nba = ba_t.shape[0]
    hrows = _round_up(sc - 1, SUBLANES)
    tok = lambda col0: pl.BlockSpec((ts, width), lambda b, h, s: (b * ns + s, col0 // width + h))
    cw = lambda col0: pl.BlockSpec((sc, width), lambda b, h, s: (0, col0 // width + h))
    smem = pl.BlockSpec(memory_space=pltpu.SMEM)
    return pl.pallas_call(
        functools.partial(_delta_kernel, heads=heads, hg=hg, sc=sc),
        grid=(batch, heads // hg, ns),
        in_specs=[
            smem, smem,
            tok(2 * cc), tok(2 * cc + dn), tok(2 * cc + 2 * dn), tok(2 * cc + 3 * dn),
            pl.BlockSpec((nba, ts), lambda b, h, s: (0, b * ns + s)),
            cw(0), cw(dn), cw(2 * dn),
            pl.BlockSpec((1, dh), lambda b, h, s: (0, 0)),
        ],
        out_specs=pl.BlockSpec((ts, width), lambda b, h, s: (b * ns + s, h)),
        out_shape=jax.ShapeDtypeStruct((t, dn), BF),
        scratch_shapes=[
            pltpu.VMEM((hg, dh, dh), F32),
            pltpu.VMEM((3, hrows, width), F32),
            pltpu.VMEM((hrows + ts, dh), F32),
            pltpu.VMEM((ts, LANES), F32),
            pltpu.VMEM((hg, CHUNK, ts), F32),
            pltpu.VMEM((hg, ts, ts), BF),
            pltpu.VMEM((hg, ts, 2 * dh), F32),
            pltpu.VMEM((hg, ts, dh), F32),
            pltpu.VMEM((hg, dh, ts), F32),
            pltpu.VMEM((hg, g, dh + CHUNK, dh), BF),
            pltpu.VMEM((hg, g, dh, dh), F32),
            pltpu.VMEM((hg, ts, dh), F32),
        ],
        compiler_params=_params(("arbitrary", "arbitrary", "arbitrary")),
        name="delta_group",
    )(a_log, dt_bias, proj, proj, proj, proj, ba_t, conv_w, conv_w, conv_w, nw.reshape(1, dh))


def _mlp_kernel(x_hbm, shift_ref, scale_ref, gate_ref, nw_ref, wu_ref, wd_ref,
                fshift_ref, fscale_ref, fnw_ref, o_ref, h_ref, xbuf_ref, xsem, *, final):
    i = pl.program_id(0)
    j = pl.program_id(1)
    tm = xbuf_ref.shape[0]

    def x_copy(tile):
        return pltpu.make_async_copy(x_hbm.at[pl.ds(tile * tm, tm), :], xbuf_ref, xsem)

    @pl.when(j == 0)
    def _():
        @pl.when(i == 0)
        def _():
            x_copy(0).start()

        x_copy(i).wait()
        _norm_mod_rows(xbuf_ref, nw_ref, shift_ref, scale_ref, h_ref, 16)
        o_ref[...] = xbuf_ref[...]

    @pl.when((j == 1) & (i + 1 < pl.num_programs(0)))
    def _():
        x_copy(i + 1).start()

    a = jnp.maximum(jnp.dot(h_ref[...], wu_ref[...], preferred_element_type=F32), 0.0)
    o_ref[...] += gate_ref[...] * jnp.dot((a * a).astype(BF), wd_ref[...], preferred_element_type=F32)

    if final:
        @pl.when(j == pl.num_programs(1) - 1)
        def _():
            _norm_mod_rows(o_ref, fnw_ref, fshift_ref, fscale_ref, o_ref, 16)


def _mlp(x2, mod4, nw, w_up16, w_down16, layer, modf, fnw, final, seq):
    t, d = x2.shape
    f = w_up16.shape[-1]
    tm = _pick(seq, 1024)
    tf = _pick(f, 1024)
    assert f // tf >= 2
    per_b = seq // tm
    modspec = lambda which: pl.BlockSpec((None, 1, d), lambda i, j: (i // per_b, 0, which))
    return pl.pallas_call(
        functools.partial(_mlp_kernel, final=final),
        grid=(t // tm, f // tf),
        in_specs=[
            pl.BlockSpec(memory_space=pl.ANY),
            modspec(3), modspec(4), modspec(5),
            pl.BlockSpec((1, d), lambda i, j: (0, 0)),
            pl.BlockSpec((None, d, tf), lambda i, j: (layer, 0, j)),
            pl.BlockSpec((None, tf, d), lambda i, j: (layer, j, 0)),
            pl.BlockSpec((None, 1, d), lambda i, j: (i // per_b, 0, 0)),
            pl.BlockSpec((None, 1, d), lambda i, j: (i // per_b, 0, 1)),
            pl.BlockSpec((1, d), lambda i, j: (0, 0)),
        ],
        out_specs=pl.BlockSpec((tm, d), lambda i, j: (i, 0)),
        out_shape=jax.ShapeDtypeStruct((t, d), F32),
        scratch_shapes=[
            pltpu.VMEM((tm, d), BF),
            pltpu.VMEM((tm, d), F32),
            pltpu.SemaphoreType.DMA(()),
        ],
        compiler_params=_params(("arbitrary", "arbitrary")),
        name="mlp",
    )(x2, mod4, mod4, mod4, nw, w_up16, w_down16, modf, modf, fnw)


def kernel(x, c, w_ada, b_ada, norm1_w, w_in, conv_dw_w, conv_dw_b, conv_ln_w, conv_ln_b, w_pw2, conv_out_norm_w, qkv_conv_w, a_log, dt_bias, dn_norm_w, w_out, norm2_w, w_up, w_down, final_ada_w, final_ada_b, final_norm_w):
    batch, seq, d = x.shape
    depth = w_ada.shape[0]
    cc = conv_dw_w.shape[-1]
    heads = a_log.shape[-1]
    dh = dn_norm_w.shape[-1]
    dn = heads * dh
    n_main = 2 * cc + 4 * dn
    assert seq % CHUNK == 0 and dh % LANES == 0 and cc % LANES == 0
    assert w_in.shape[-1] == n_main + 2 * heads

    mods = _ada(c, w_ada, b_ada)
    modf = _ada(c, final_ada_w[None], final_ada_b[None])[0]
    modf = modf.reshape(batch, 1, 2 * d)

    (w_in16, w_ba16), w_pw16, w_out16 = _to_bf16(w_in, n_main), _to_bf16(w_pw2), _to_bf16(w_out)
    w_up16, w_down16 = _to_bf16(w_up), _to_bf16(w_down)
    w_ba_t = jnp.swapaxes(w_ba16, 1, 2)

    x2 = x.reshape(batch * seq, d)
    for l in range(depth):
        mod4 = mods[l].reshape(batch, 1, 6 * d)
        proj, ba_t = _inproj(x2, mod4, norm1_w[l].reshape(1, d), w_in16, w_ba_t[l], l, n_main, seq)
        y_dn = _delta_group(proj, ba_t, qkv_conv_w[l], a_log[l], dt_bias[l], dn_norm_w[l],
                            batch, seq, cc, heads, dh)
        x2 = _mix_out(proj, y_dn, x2, conv_dw_w[l], conv_dw_b[l], conv_ln_w[l], conv_ln_b[l],
                      w_pw16, w_out16, l, conv_out_norm_w[l], mod4, seq)
        x2 = _mlp(x2, mod4, norm2_w[l].reshape(1, d), w_up16, w_down16, l,
                  modf, final_norm_w.reshape(1, d), l == depth - 1, seq)
    return x2.reshape(batch, seq, d)
```

```python
import functools

import jax
import jax.numpy as jnp
from jax import lax
from jax.experimental import pallas as pl
from jax.experimental.pallas import tpu as pltpu

EPS = 1e-6
CHUNK = 64
SUBLANES = 8
LANES = 128
VMEM_LIMIT = 56 * 1024 * 1024
CAST_BLOCK_BYTES = 8 * 1024 * 1024
BF = jnp.bfloat16
F32 = jnp.float32


def _round_up(n, m):
    return (n + m - 1) // m * m


def _pick(n, pref):
    t = min(n, pref)
    while n % t:
        t //= 2
    return t


def _params(sem):
    return pltpu.CompilerParams(dimension_semantics=sem, vmem_limit_bytes=VMEM_LIMIT)


def _sigmoid(v):
    return jax.nn.sigmoid(v)


def _silu(v):
    return v * jax.nn.sigmoid(v)


def _cast_kernel(w_ref, o_ref):
    o_ref[...] = w_ref[...].astype(o_ref.dtype)


def _to_bf16(w):
    nl, k, n = w.shape
    pref = max(2 * SUBLANES, CAST_BLOCK_BYTES // (4 * n))
    aligned = [t for t in range(2 * SUBLANES, pref + 1, 2 * SUBLANES) if k % t == 0]
    tk = max(aligned) if aligned else k
    return pl.pallas_call(
        _cast_kernel,
        grid=(nl, k // tk),
        in_specs=[pl.BlockSpec((None, tk, n), lambda l, i: (l, i, 0))],
        out_specs=pl.BlockSpec((None, tk, n), lambda l, i: (l, i, 0)),
        out_shape=jax.ShapeDtypeStruct((nl, k, n), BF),
        compiler_params=_params(("parallel", "parallel")),
        name="to_bf16",
    )(w)


def _ada_kernel(c_ref, w_ref, b_ref, o_ref):
    ca = _silu(c_ref[...]).astype(BF)
    o_ref[...] = jnp.dot(ca, w_ref[...].astype(BF), preferred_element_type=F32) + b_ref[...]


def _ada(c, w, b):
    nl, d, n = w.shape
    nb = c.shape[0]
    tn = _pick(n, 1024)
    return pl.pallas_call(
        _ada_kernel,
        grid=(nl, n // tn),
        in_specs=[
            pl.BlockSpec((nb, d), lambda l, j: (0, 0)),
            pl.BlockSpec((None, d, tn), lambda l, j: (l, 0, j)),
            pl.BlockSpec((None, 1, tn), lambda l, j: (l, 0, j)),
        ],
        out_specs=pl.BlockSpec((None, nb, tn), lambda l, j: (l, 0, j)),
        out_shape=jax.ShapeDtypeStruct((nl, nb, n), F32),
        compiler_params=_params(("parallel", "parallel")),
        name="ada",
    )(c, w, b.reshape(nl, 1, n))


def _norm_mod_rows(x_ref, nw_ref, shift_ref, scale_ref, h_ref, rows_per_step):
    tm = x_ref.shape[0]
    gain = nw_ref[...] * (1.0 + scale_ref[...])
    shift = shift_ref[...]
    for r in range(tm // rows_per_step):
        rows = pl.ds(r * rows_per_step, rows_per_step)
        xr = x_ref[rows, :]
        inv = lax.rsqrt(jnp.mean(xr * xr, axis=-1, keepdims=True) + EPS)
        h_ref[rows, :] = (x_ref[rows, :] * inv * gain + shift).astype(h_ref.dtype)


def _inproj_kernel(x_ref, shift_ref, scale_ref, nw_ref, w_ref, wba_ref, proj_ref, ba_ref, h_ref):
    @pl.when(pl.program_id(1) == 0)
    def _():
        _norm_mod_rows(x_ref, nw_ref, shift_ref, scale_ref, h_ref, 16)
        ba_ref[...] = lax.dot_general(wba_ref[...], h_ref[...], (((1,), (1,)), ((), ())),
                                      preferred_element_type=F32)

    proj_ref[...] = lax.dot_general(h_ref[...], w_ref[...], (((1,), (1,)), ((), ())),
                                    preferred_element_type=F32)


def _inproj(x2, mod4, nw, w_in_t16, layer, n, seq):
    t, d = x2.shape
    nba = w_in_t16.shape[1] - n
    assert n % nba == 0
    tm = _pick(seq, 1024)
    tn = _pick(n, 1024)
    per_b = seq // tm
    return pl.pallas_call(
        _inproj_kernel,
        grid=(t // tm, n // tn),
        in_specs=[
            pl.BlockSpec((tm, d), lambda i, j: (i, 0)),
            pl.BlockSpec((None, 1, d), lambda i, j: (i // per_b, 0, 0)),
            pl.BlockSpec((None, 1, d), lambda i, j: (i // per_b, 0, 1)),
            pl.BlockSpec((1, d), lambda i, j: (0, 0)),
            pl.BlockSpec((None, tn, d), lambda i, j: (layer, j, 0)),
            pl.BlockSpec((None, nba, d), lambda i, j: (layer, n // nba, 0)),
        ],
        out_specs=[
            pl.BlockSpec((tm, tn), lambda i, j: (i, j)),
            pl.BlockSpec((nba, tm), lambda i, j: (0, i)),
        ],
        out_shape=[
            jax.ShapeDtypeStruct((t, n), F32),
            jax.ShapeDtypeStruct((nba, t), F32),
        ],
        scratch_shapes=[pltpu.VMEM((tm, d), BF)],
        compiler_params=_params(("parallel", "arbitrary")),
        name="inproj",
    )(x2, mod4, mod4, nw, w_in_t16, w_in_t16)


def _mix_out_kernel(val_ref, gate_ref, dww_ref, dwb_ref, lnw_ref, lnb_ref, pw_ref, onw_ref,
                    ydn_ref, x_ref, wc_ref, wd_ref, g1_ref, o_ref,
                    ubuf_ref, cbuf_ref, abuf_ref, yc_ref, *, kw, halo, rb, cr, per_b):
    ts, cc = val_ref.shape
    ncol = cc // LANES
    s = pl.program_id(0)

    @pl.when(s == 0)
    def _():
        yc_ref[...] = jnp.zeros_like(yc_ref)

    @pl.when(s % per_b == 0)
    def _():
        ubuf_ref[:, pl.ds(0, halo), :] = jnp.zeros((ncol, halo, LANES), F32)

    for cb in range(ncol):
        lanes = slice(cb * LANES, (cb + 1) * LANES)
        for r0 in range(0, ts, cr):
            rows = pl.ds(r0, cr)
            ubuf_ref[cb, pl.ds(halo + r0, cr), :] = val_ref[rows, lanes] * _sigmoid(gate_ref[rows, lanes])

    first = halo - (kw - 1)
    nblk = o_ref.shape[1] // ncol
    for cb in range(ncol):
        ocols = slice(cb * nblk, (cb + 1) * nblk)
        y = (jnp.dot(yc_ref[(s + 1) % 2], wc_ref[:, ocols], preferred_element_type=F32)
             + jnp.dot(ydn_ref[...], wd_ref[:, ocols], preferred_element_type=F32))
        o_ref[:, ocols] = x_ref[:, ocols] + g1_ref[:, ocols] * y

        lanes = slice(cb * LANES, (cb + 1) * LANES)
        for r0 in range(0, ts, cr):
            acc = jnp.broadcast_to(dwb_ref[:, lanes], (cr, LANES))
            for k in range(kw):
                acc = acc + ubuf_ref[cb, pl.ds(r0 + first + k, cr), :] * dww_ref[pl.ds(k, 1), lanes]
            cbuf_ref[pl.ds(r0, cr), lanes] = acc

    lnw = lnw_ref[...]
    lnb = lnb_ref[...]
    for r in range(ts // rb):
        rows = pl.ds(r * rb, rb)
        acc = cbuf_ref[rows, :]
        mu = jnp.mean(acc, axis=-1, keepdims=True)
        xc = acc - mu
        var = jnp.mean(xc * xc, axis=-1, keepdims=True)
        yn = xc * lax.rsqrt(var + EPS) * lnw + lnb
        abuf_ref[rows, :] = _silu(yn).astype(BF)

    ubuf_ref[:, pl.ds(0, halo), :] = ubuf_ref[:, pl.ds(ts, halo), :]

    p = jnp.dot(abuf_ref[...], pw_ref[...], preferred_element_type=F32)
    inv = lax.rsqrt(jnp.mean(p * p, axis=-1, keepdims=True) + EPS)
    yc_ref[s % 2] = (p * inv * onw_ref[...]).astype(BF)


def _mix_out(proj, y_dn, x2, dww, dwb, lnw, lnb, pw16, w_out16, layer, onw, mod4, seq):
    kw, cc = dww.shape
    t, d = x2.shape
    dn = y_dn.shape[1]
    assert cc % dn == 0
    halo = _round_up(kw - 1, SUBLANES)
    ts = _pick(seq, 512)
    per_b = seq // ts
    nt = t // ts
    row = lambda a: a.reshape(1, cc)
    cur = lambda s: jnp.minimum(s, nt - 1)
    prev = lambda s: jnp.maximum(s - 1, 0)
    const = lambda shape: pl.BlockSpec(shape, lambda s: (0,) * len(shape))
    return pl.pallas_call(
        functools.partial(_mix_out_kernel, kw=kw, halo=halo, rb=16, cr=_pick(ts, 64), per_b=per_b),
        grid=(nt + 1,),
        in_specs=[
            pl.BlockSpec((ts, cc), lambda s: (cur(s), 0)),
            pl.BlockSpec((ts, cc), lambda s: (cur(s), 1)),
            const((kw, cc)), const((1, cc)), const((1, cc)), const((1, cc)),
            pl.BlockSpec((None, cc, cc), lambda s: (layer, 0, 0)),
            const((1, cc)),
            pl.BlockSpec((ts, dn), lambda s: (prev(s), 0)),
            pl.BlockSpec((ts, d), lambda s: (prev(s), 0)),
            pl.BlockSpec((None, cc, d), lambda s: (layer, 0, 0)),
            pl.BlockSpec((None, dn, d), lambda s: (layer, cc // dn, 0)),
            pl.BlockSpec((None, 1, d), lambda s: (prev(s) // per_b, 0, 2)),
        ],
        out_specs=pl.BlockSpec((ts, d), lambda s: (prev(s), 0)),
        out_shape=jax.ShapeDtypeStruct((t, d), F32),
        scratch_shapes=[
            pltpu.VMEM((cc // LANES, halo + ts, LANES), F32),
            pltpu.VMEM((ts, cc), F32),
            pltpu.VMEM((ts, cc), BF),
            pltpu.VMEM((2, ts, cc), BF),
        ],
        compiler_params=_params(("arbitrary",)),
        name="mix_out",
    )(proj, proj, dww, row(dwb), row(lnw), row(lnb), pw16, row(onw), y_dn, x2, w_out16, w_out16, mod4)


def _dot_nt(a, b):
    return lax.dot_general(a, b, (((1,), (1,)), ((), ())), preferred_element_type=F32)


def _dot(a, b):
    return jnp.dot(a, b, preferred_element_type=F32)


def _blockdiag(xp, same_chunk):
    g = xp.shape[1] // CHUNK
    return jnp.where(same_chunk, jnp.concatenate([xp] * g, axis=0), 0.0).astype(BF)


def _packed_mm(xs, y, same_chunk):
    a = _dot(jnp.concatenate(list(xs), axis=0).astype(BF), _blockdiag(y, same_chunk))
    return [a[i * CHUNK:(i + 1) * CHUNK] for i in range(len(xs))]


def _packed_unit_lower_inverse(lps, same_chunk):
    ts = lps[0].shape[1]
    pi = lax.broadcasted_iota(jnp.int32, (CHUNK, ts), 0)
    pj = lax.broadcasted_iota(jnp.int32, (CHUNK, ts), 1) % CHUNK
    eye = (pi == pj).astype(F32)
    same16 = (pi // 16) == (pj // 16)
    same32 = (pi // 32) == (pj // 32)
    mm = lambda xs_list, ys: [_packed_mm(xs, y, same_chunk) for xs, y in zip(xs_list, ys)]

    l16 = [jnp.where(same16, lp, 0.0) for lp in lps]
    l2 = [r[0] for r in mm([[a] for a in l16], l16)]
    x = [eye - a for a in l16]
    r = mm([[a, b] for a, b in zip(x, l2)], l2)
    x = [a + b[0] for a, b in zip(x, r)]
    l4 = [b[1] for b in r]
    r = mm([[a, b] for a, b in zip(x, l4)], l4)
    x = [a + b[0] for a, b in zip(x, r)]
    l8 = [b[1] for b in r]
    r = mm([[a] for a in x], l8)
    x = [a + b[0] for a, b in zip(x, r)]
    for keep in (same32 & jnp.logical_not(same16), jnp.logical_not(same32)):
        cm = [jnp.where(keep, lp, 0.0) for lp in lps]
        t1 = [b[0] for b in mm([[a] for a in x], cm)]
        t2 = [b[0] for b in mm([[a] for a in t1], x)]
        x = [a - b for a, b in zip(x, t2)]
    return x


def _delta_kernel(alog_ref, dtb_ref, q_ref, k_ref, v_ref, z_ref, ba_ref, wq_ref, wk_ref, wv_ref,
                  nw_ref, y_ref, state_ref, halo_ref, ext_ref, cols_ref, lp_ref, attn_ref, rhs_ref,
                  qd_ref, kdt_ref, lhs_ref, n_ref, o0_ref, *, heads, hg, sc):
    ts, width = q_ref.shape
    dh = width // hg
    g = ts // CHUNK
    head0 = pl.program_id(1) * hg
    hrows = halo_ref.shape[1]

    @pl.when(pl.program_id(2) == 0)
    def _():
        state_ref[...] = jnp.zeros_like(state_ref)
        halo_ref[...] = jnp.zeros_like(halo_ref)

    pad = [jnp.zeros((SUBLANES - hg, ts), F32)] if hg < SUBLANES else []
    betas, gs = [], []
    for i in range(hg):
        b_logit = ba_ref[pl.ds(head0 + i, 1), :]
        xs = ba_ref[pl.ds(heads + head0 + i, 1), :] + dtb_ref[head0 + i]
        softplus = jnp.maximum(xs, 0.0) + jnp.log(1.0 + jnp.exp(-jnp.abs(xs)))
        betas.append(_sigmoid(b_logit))
        gs.append(-jnp.exp(jnp.full((1, ts), alog_ref[head0 + i], F32)) * softplus)
    beta_rows = jnp.concatenate(betas + pad, axis=0)
    g_rows = jnp.concatenate(gs + pad, axis=0)
    lane = lax.broadcasted_iota(jnp.int32, (SUBLANES, ts), 1) % CHUNK
    d = 1
    while d < CHUNK:
        g_rows = g_rows + jnp.where(lane >= d, pltpu.roll(g_rows, d, 1), 0.0)
        d *= 2
    gl_rows = jnp.where(lane == CHUNK - 1, g_rows, 0.0)
    d = 1
    while d < CHUNK:
        gl_rows = gl_rows + jnp.where(lane + d < CHUNK, pltpu.roll(gl_rows, ts - d, 1), 0.0)
        d *= 2
    kdec_rows = jnp.exp(gl_rows - g_rows)
    rows = jnp.concatenate([beta_rows, g_rows, jnp.zeros((LANES - 2 * SUBLANES, ts), F32)], axis=0)
    cols_ref[...] = rows.T

    ri = lax.broadcasted_iota(jnp.int32, (ts, ts), 0)
    ci = lax.broadcasted_iota(jnp.int32, (ts, ts), 1)
    same_chunk = (ri // CHUNK) == (ci // CHUNK)
    causal = same_chunk & (ri >= ci)

    def short_conv(idx, x_ref, w_ref, lanes):
        ext_ref[pl.ds(0, hrows), :] = halo_ref[idx, :, lanes]
        ext_ref[pl.ds(hrows, ts), :] = x_ref[:, lanes]
        acc = jnp.zeros((ts, dh), F32)
        for j in range(sc):
            acc = acc + ext_ref[pl.ds(hrows - (sc - 1) + j, ts), :] * w_ref[pl.ds(j, 1), lanes]
        halo_ref[idx, :, lanes] = x_ref[pl.ds(ts - hrows, hrows), lanes]
        return _silu(acc)

    for i in range(hg):
        lanes = slice(i * dh, (i + 1) * dh)
        q = short_conv(0, q_ref, wq_ref, lanes)
        k = short_conv(1, k_ref, wk_ref, lanes)
        v = short_conv(2, v_ref, wv_ref, lanes)
        q = q * lax.rsqrt(jnp.sum(q * q, axis=-1, keepdims=True) + EPS) * (dh ** -0.5)
        k = k * lax.rsqrt(jnp.sum(k * k, axis=-1, keepdims=True) + EPS)
        beta_col = cols_ref[:, i:i + 1]
        g_col = cols_ref[:, SUBLANES + i:SUBLANES + i + 1]
        g_row = g_rows[i:i + 1]
        eg = jnp.exp(g_col)
        kb = k * beta_col
        rhs_ref[i] = jnp.concatenate([v * beta_col, kb * eg], axis=1)
        qd_ref[i] = q * eg
        decay = jnp.exp(jnp.where(causal, g_col - g_row, -jnp.inf))
        kq = _dot_nt(jnp.concatenate([kb, q], axis=0).astype(BF), k.astype(BF))
        attn_ref[i] = (kq[ts:] * decay).astype(BF)
        lfull = jnp.where(ri > ci, kq[:ts] * decay, 0.0)
        lp = lfull[0:CHUNK]
        for c in range(1, g):
            lp = lp + lfull[c * CHUNK:(c + 1) * CHUNK]
        lp_ref[i] = lp
        kdt_ref[i] = k.T * kdec_rows[i:i + 1]

    tinv = _packed_unit_lower_inverse([lp_ref[i] for i in range(hg)], same_chunk)

    cj = lax.broadcasted_iota(jnp.int32, (dh, ts), 1) // CHUNK
    eye_p = (lax.broadcasted_iota(jnp.int32, (CHUNK, ts), 0)
             == lax.broadcasted_iota(jnp.int32, (CHUNK, ts), 1) % CHUNK).astype(F32)
    for i in range(hg):
        t_off = _blockdiag(tinv[i] - eye_p, same_chunk)
        rhs = rhs_ref[i]
        uw = rhs + _dot(t_off, rhs.astype(BF))
        uw16 = uw.astype(BF)
        auw = _dot(attn_ref[i], uw16)
        o0_ref[i] = auw[:, :dh]
        qp16 = (qd_ref[i] - auw[:, dh:]).astype(BF)
        kdt = kdt_ref[i]
        for c in range(g):
            kun = _dot(jnp.where(cj == c, kdt, 0.0).astype(BF), uw16)
            n_ref[i, c] = kun[:, :dh]
            lhs_ref[i, c] = jnp.concatenate(
                [kun[:, dh:].astype(BF), qp16[c * CHUNK:(c + 1) * CHUNK]], axis=0)

    nw = nw_ref[...]
    for c in range(g):
        r0 = c * CHUNK
        for i in range(hg):
            lanes = slice(i * dh, (i + 1) * dh)
            state = state_ref[i]
            r = _dot(lhs_ref[i, c], state.astype(BF))
            o = r[dh:] + o0_ref[i, pl.ds(r0, CHUNK), :]
            tail = cols_ref[pl.ds(r0 + CHUNK - SUBLANES, SUBLANES), :]
            g_last = tail[SUBLANES - 1:SUBLANES, SUBLANES + i:SUBLANES + i + 1]
            state_ref[i] = state * jnp.exp(g_last) - r[:dh] + n_ref[i, c]
            on = o * lax.rsqrt(jnp.mean(o * o, axis=-1, keepdims=True) + EPS) * nw
            y_ref[pl.ds(r0, CHUNK), lanes] = (on * _silu(z_ref[pl.ds(r0, CHUNK), lanes])).astype(y_ref.dtype)


def _delta_group(proj, ba_t, conv_w, a_log, dt_bias, nw, batch, seq, cc, heads, dh):
    sc = conv_w.shape[0]
    dn = heads * dh
    hg = _pick(heads, 8)
    width = hg * dh
    assert (2 * cc) % width == 0 and hg <= SUBLANES
    ts = _pick(seq, 256)
    ns = seq // ts
    g = ts // CHUNK
    t = batch * seq
    nba = ba_t.shape[0]
    hrows = _round_up(sc - 1, SUBLANES)
    tok = lambda col0: pl.BlockSpec((ts, width), lambda b, h, s: (b * ns + s, col0 // width + h))
    cw = lambda col0: pl.BlockSpec((sc, width), lambda b, h, s: (0, col0 // width + h))
    smem = pl.BlockSpec(memory_space=pltpu.SMEM)
    return pl.pallas_call(
        functools.partial(_delta_kernel, heads=heads, hg=hg, sc=sc),
        grid=(batch, heads // hg, ns),
        in_specs=[
            smem, smem,
            tok(2 * cc), tok(2 * cc + dn), tok(2 * cc + 2 * dn), tok(2 * cc + 3 * dn),
            pl.BlockSpec((nba, ts), lambda b, h, s: (0, b * ns + s)),
            cw(0), cw(dn), cw(2 * dn),
            pl.BlockSpec((1, dh), lambda b, h, s: (0, 0)),
        ],
        out_specs=pl.BlockSpec((ts, width), lambda b, h, s: (b * ns + s, h)),
        out_shape=jax.ShapeDtypeStruct((t, dn), BF),
        scratch_shapes=[
            pltpu.VMEM((hg, dh, dh), F32),
            pltpu.VMEM((3, hrows, width), F32),
            pltpu.VMEM((hrows + ts, dh), F32),
            pltpu.VMEM((ts, LANES), F32),
            pltpu.VMEM((hg, CHUNK, ts), F32),
            pltpu.VMEM((hg, ts, ts), BF),
            pltpu.VMEM((hg, ts, 2 * dh), F32),
            pltpu.VMEM((hg, ts, dh), F32),
            pltpu.VMEM((hg, dh, ts), F32),
            pltpu.VMEM((hg, g, dh + CHUNK, dh), BF),
            pltpu.VMEM((hg, g, dh, dh), F32),
            pltpu.VMEM((hg, ts, dh), F32),
        ],
        compiler_params=_params(("arbitrary", "arbitrary", "arbitrary")),
        name="delta_group",
    )(a_log, dt_bias, proj, proj, proj, proj, ba_t, conv_w, conv_w, conv_w, nw.reshape(1, dh))


def _mlp_kernel(x_hbm, shift_ref, scale_ref, gate_ref, nw_ref, wu_ref, wd_ref,
                fshift_ref, fscale_ref, fnw_ref, o_ref, h_ref, xbuf_ref, xsem, *, final):
    i = pl.program_id(0)
    j = pl.program_id(1)
    tm = xbuf_ref.shape[0]

    def x_copy(tile):
        return pltpu.make_async_copy(x_hbm.at[pl.ds(tile * tm, tm), :], xbuf_ref, xsem)

    @pl.when(j == 0)
    def _():
        @pl.when(i == 0)
        def _():
            x_copy(0).start()

        x_copy(i).wait()
        _norm_mod_rows(xbuf_ref, nw_ref, shift_ref, scale_ref, h_ref, 16)
        o_ref[...] = xbuf_ref[...]

    @pl.when((j == 1) & (i + 1 < pl.num_programs(0)))
    def _():
        x_copy(i + 1).start()

    a = jnp.maximum(jnp.dot(h_ref[...], wu_ref[...], preferred_element_type=F32), 0.0)
    o_ref[...] += gate_ref[...] * jnp.dot((a * a).astype(BF), wd_ref[...], preferred_element_type=F32)

    if final:
        @pl.when(j == pl.num_programs(1) - 1)
        def _():
            _norm_mod_rows(o_ref, fnw_ref, fshift_ref, fscale_ref, o_ref, 16)


def _mlp(x2, mod4, nw, w_up16, w_down16, layer, modf, fnw, final, seq):
    t, d = x2.shape
    f = w_up16.shape[-1]
    tm = _pick(seq, 1024)
    tf = _pick(f, 1024)
    assert f // tf >= 2
    per_b = seq // tm
    modspec = lambda which: pl.BlockSpec((None, 1, d), lambda i, j: (i // per_b, 0, which))
    return pl.pallas_call(
        functools.partial(_mlp_kernel, final=final),
        grid=(t // tm, f // tf),
        in_specs=[
            pl.BlockSpec(memory_space=pl.ANY),
            modspec(3), modspec(4), modspec(5),
            pl.BlockSpec((1, d), lambda i, j: (0, 0)),
            pl.BlockSpec((None, d, tf), lambda i, j: (layer, 0, j)),
            pl.BlockSpec((None, tf, d), lambda i, j: (layer, j, 0)),
            pl.BlockSpec((None, 1, d), lambda i, j: (i // per_b, 0, 0)),
            pl.BlockSpec((None, 1, d), lambda i, j: (i // per_b, 0, 1)),
            pl.BlockSpec((1, d), lambda i, j: (0, 0)),
        ],
        out_specs=pl.BlockSpec((tm, d), lambda i, j: (i, 0)),
        out_shape=jax.ShapeDtypeStruct((t, d), F32),
        scratch_shapes=[
            pltpu.VMEM((tm, d), BF),
            pltpu.VMEM((tm, d), F32),
            pltpu.SemaphoreType.DMA(()),
        ],
        compiler_params=_params(("arbitrary", "arbitrary")),
        name="mlp",
    )(x2, mod4, mod4, mod4, nw, w_up16, w_down16, modf, modf, fnw)


def kernel(x, c, w_ada, b_ada, norm1_w, w_in, conv_dw_w, conv_dw_b, conv_ln_w, conv_ln_b, w_pw2, conv_out_norm_w, qkv_conv_w, a_log, dt_bias, dn_norm_w, w_out, norm2_w, w_up, w_down, final_ada_w, final_ada_b, final_norm_w):
    batch, seq, d = x.shape
    depth = w_ada.shape[0]
    cc = conv_dw_w.shape[-1]
    heads = a_log.shape[-1]
    dh = dn_norm_w.shape[-1]
    dn = heads * dh
    n_main = 2 * cc + 4 * dn
    assert seq % CHUNK == 0 and dh % LANES == 0 and cc % LANES == 0
    assert w_in.shape[-1] == n_main + 2 * heads

    mods = _ada(c, w_ada, b_ada)
    modf = _ada(c, final_ada_w[None], final_ada_b[None])[0]
    modf = modf.reshape(batch, 1, 2 * d)

    w_in_t16 = _to_bf16(jnp.swapaxes(w_in, 1, 2))
    w_pw16, w_out16, w_up16, w_down16 = _to_bf16(w_pw2), _to_bf16(w_out), _to_bf16(w_up), _to_bf16(w_down)

    x2 = x.reshape(batch * seq, d)
    for l in range(depth):
        mod4 = mods[l].reshape(batch, 1, 6 * d)
        proj, ba_t = _inproj(x2, mod4, norm1_w[l].reshape(1, d), w_in_t16, l, n_main, seq)
        y_dn = _delta_group(proj, ba_t, qkv_conv_w[l], a_log[l], dt_bias[l], dn_norm_w[l],
                            batch, seq, cc, heads, dh)
        x2 = _mix_out(proj, y_dn, x2, conv_dw_w[l], conv_dw_b[l], conv_ln_w[l], conv_ln_b[l],
                      w_pw16, w_out16, l, conv_out_norm_w[l], mod4, seq)
        x2 = _mlp(x2, mod4, norm2_w[l].reshape(1, d), w_up16, w_down16, l,
                  modf, final_norm_w.reshape(1, d), l == depth - 1, seq)
    return x2.reshape(batch, seq, d)
```

```python
import functools

import jax
import jax.numpy as jnp
from jax import lax
from jax.experimental import pallas as pl
from jax.experimental.pallas import tpu as pltpu

EPS = 1e-6
CHUNK = 64
SUBLANES = 8
LANES = 128
VMEM_LIMIT = 56 * 1024 * 1024
CAST_BLOCK_BYTES = 8 * 1024 * 1024
BF = jnp.bfloat16
F32 = jnp.float32


def _round_up(n, m):
    return (n + m - 1) // m * m


def _pick(n, pref):
    t = min(n, pref)
    while n % t:
        t //= 2
    return t


def _params(sem):
    return pltpu.CompilerParams(dimension_semantics=sem, vmem_limit_bytes=VMEM_LIMIT)


def _sigmoid(v):
    return jax.nn.sigmoid(v)


def _silu(v):
    return v * jax.nn.sigmoid(v)


def _cast_kernel(w_ref, o_ref):
    o_ref[...] = w_ref[...].astype(o_ref.dtype)


def _to_bf16(w):
    nl, k, n = w.shape
    pref = max(2 * SUBLANES, CAST_BLOCK_BYTES // (4 * n))
    aligned = [t for t in range(2 * SUBLANES, pref + 1, 2 * SUBLANES) if k % t == 0]
    tk = max(aligned) if aligned else k
    return pl.pallas_call(
        _cast_kernel,
        grid=(nl, k // tk),
        in_specs=[pl.BlockSpec((None, tk, n), lambda l, i: (l, i, 0))],
        out_specs=pl.BlockSpec((None, tk, n), lambda l, i: (l, i, 0)),
        out_shape=jax.ShapeDtypeStruct((nl, k, n), BF),
        compiler_params=_params(("parallel", "parallel")),
        name="to_bf16",
    )(w)


def _ada_kernel(c_ref, w_ref, b_ref, o_ref):
    ca = _silu(c_ref[...]).astype(BF)
    o_ref[...] = jnp.dot(ca, w_ref[...].astype(BF), preferred_element_type=F32) + b_ref[...]


def _ada(c, w, b):
    nl, d, n = w.shape
    nb = c.shape[0]
    tn = _pick(n, 1024)
    return pl.pallas_call(
        _ada_kernel,
        grid=(nl, n // tn),
        in_specs=[
            pl.BlockSpec((nb, d), lambda l, j: (0, 0)),
            pl.BlockSpec((None, d, tn), lambda l, j: (l, 0, j)),
            pl.BlockSpec((None, 1, tn), lambda l, j: (l, 0, j)),
        ],
        out_specs=pl.BlockSpec((None, nb, tn), lambda l, j: (l, 0, j)),
        out_shape=jax.ShapeDtypeStruct((nl, nb, n), F32),
        compiler_params=_params(("parallel", "parallel")),
        name="ada",
    )(c, w, b.reshape(nl, 1, n))


def _norm_mod_rows(x_ref, nw_ref, shift_ref, scale_ref, h_ref, rows_per_step):
    tm = x_ref.shape[0]
    gain = nw_ref[...] * (1.0 + scale_ref[...])
    shift = shift_ref[...]
    for r in range(tm // rows_per_step):
        rows = pl.ds(r * rows_per_step, rows_per_step)
        xr = x_ref[rows, :]
        inv = lax.rsqrt(jnp.mean(xr * xr, axis=-1, keepdims=True) + EPS)
        h_ref[rows, :] = (x_ref[rows, :] * inv * gain + shift).astype(h_ref.dtype)


def _inproj_kernel(x_hbm, shift_ref, scale_ref, nw_ref, w_ref, wba_ref, proj_ref, ba_ref,
                   h_ref, xbuf_ref, xsem):
    i = pl.program_id(0)
    j = pl.program_id(1)
    tm = xbuf_ref.shape[0]

    def x_copy(tile):
        return pltpu.make_async_copy(x_hbm.at[pl.ds(tile * tm, tm), :], xbuf_ref, xsem)

    @pl.when(j == 0)
    def _():
        @pl.when(i == 0)
        def _():
            x_copy(0).start()

        x_copy(i).wait()
        _norm_mod_rows(xbuf_ref, nw_ref, shift_ref, scale_ref, h_ref, 16)
        ba_ref[...] = lax.dot_general(wba_ref[...], h_ref[...], (((1,), (1,)), ((), ())),
                                      preferred_element_type=F32)

    @pl.when((j == 1) & (i + 1 < pl.num_programs(0)))
    def _():
        x_copy(i + 1).start()

    proj_ref[...] = lax.dot_general(h_ref[...], w_ref[...], (((1,), (1,)), ((), ())),
                                    preferred_element_type=F32)


def _inproj(x2, mod4, nw, w_in_t16, layer, n, seq):
    t, d = x2.shape
    nba = w_in_t16.shape[1] - n
    assert n % nba == 0
    tm = _pick(seq, 1024)
    tn = _pick(n, 2048)
    if n // tn < 2:
        tn = _pick(n, tn // 2)
    assert n // tn >= 2
    per_b = seq // tm
    return pl.pallas_call(
        _inproj_kernel,
        grid=(t // tm, n // tn),
        in_specs=[
            pl.BlockSpec(memory_space=pl.ANY),
            pl.BlockSpec((None, 1, d), lambda i, j: (i // per_b, 0, 0)),
            pl.BlockSpec((None, 1, d), lambda i, j: (i // per_b, 0, 1)),
            pl.BlockSpec((1, d), lambda i, j: (0, 0)),
            pl.BlockSpec((None, tn, d), lambda i, j: (layer, j, 0)),
            pl.BlockSpec((None, nba, d), lambda i, j: (layer, n // nba, 0)),
        ],
        out_specs=[
            pl.BlockSpec((tm, tn), lambda i, j: (i, j)),
            pl.BlockSpec((nba, tm), lambda i, j: (0, i)),
        ],
        out_shape=[
            jax.ShapeDtypeStruct((t, n), F32),
            jax.ShapeDtypeStruct((nba, t), F32),
        ],
        scratch_shapes=[
            pltpu.VMEM((tm, d), BF),
            pltpu.VMEM((tm, d), F32),
            pltpu.SemaphoreType.DMA(()),
        ],
        compiler_params=_params(("arbitrary", "arbitrary")),
        name="inproj",
    )(x2, mod4, mod4, nw, w_in_t16, w_in_t16)


def _mix_out_kernel(val_ref, gate_ref, dww_ref, dwb_ref, lnw_ref, lnb_ref, pw_ref, onw_ref,
                    ydn_ref, x_ref, wc_ref, wd_ref, g1_ref, o_ref,
                    ubuf_ref, cbuf_ref, abuf_ref, yc_ref, *, kw, halo, rb, cr, per_b):
    ts, cc = val_ref.shape
    ncol = cc // LANES
    s = pl.program_id(0)

    @pl.when(s == 0)
    def _():
        yc_ref[...] = jnp.zeros_like(yc_ref)

    @pl.when(s % per_b == 0)
    def _():
        ubuf_ref[:, pl.ds(0, halo), :] = jnp.zeros((ncol, halo, LANES), F32)

    for cb in range(ncol):
        lanes = slice(cb * LANES, (cb + 1) * LANES)
        for r0 in range(0, ts, cr):
            rows = pl.ds(r0, cr)
            ubuf_ref[cb, pl.ds(halo + r0, cr), :] = val_ref[rows, lanes] * _sigmoid(gate_ref[rows, lanes])

    first = halo - (kw - 1)
    nblk = o_ref.shape[1] // ncol
    for cb in range(ncol):
        ocols = slice(cb * nblk, (cb + 1) * nblk)
        y = (jnp.dot(yc_ref[(s + 1) % 2], wc_ref[:, ocols], preferred_element_type=F32)
             + jnp.dot(ydn_ref[...], wd_ref[:, ocols], preferred_element_type=F32))
        o_ref[:, ocols] = x_ref[:, ocols] + g1_ref[:, ocols] * y

        lanes = slice(cb * LANES, (cb + 1) * LANES)
        for r0 in range(0, ts, cr):
            acc = jnp.broadcast_to(dwb_ref[:, lanes], (cr, LANES))
            for k in range(kw):
                acc = acc + ubuf_ref[cb, pl.ds(r0 + first + k, cr), :] * dww_ref[pl.ds(k, 1), lanes]
            cbuf_ref[pl.ds(r0, cr), lanes] = acc

    lnw = lnw_ref[...]
    lnb = lnb_ref[...]
    for r in range(ts // rb):
        rows = pl.ds(r * rb, rb)
        acc = cbuf_ref[rows, :]
        mu = jnp.mean(acc, axis=-1, keepdims=True)
        xc = acc - mu
        var = jnp.mean(xc * xc, axis=-1, keepdims=True)
        yn = xc * lax.rsqrt(var + EPS) * lnw + lnb
        abuf_ref[rows, :] = _silu(yn).astype(BF)

    ubuf_ref[:, pl.ds(0, halo), :] = ubuf_ref[:, pl.ds(ts, halo), :]

    p = jnp.dot(abuf_ref[...], pw_ref[...], preferred_element_type=F32)
    inv = lax.rsqrt(jnp.mean(p * p, axis=-1, keepdims=True) + EPS)
    yc_ref[s % 2] = (p * inv * onw_ref[...]).astype(BF)


def _mix_out(proj, y_dn, x2, dww, dwb, lnw, lnb, pw16, w_out16, layer, onw, mod4, seq):
    kw, cc = dww.shape
    t, d = x2.shape
    dn = y_dn.shape[1]
    assert cc % dn == 0
    halo = _round_up(kw - 1, SUBLANES)
    ts = _pick(seq, 512)
    per_b = seq // ts
    nt = t // ts
    row = lambda a: a.reshape(1, cc)
    cur = lambda s: jnp.minimum(s, nt - 1)
    prev = lambda s: jnp.maximum(s - 1, 0)
    const = lambda shape: pl.BlockSpec(shape, lambda s: (0,) * len(shape))
    return pl.pallas_call(
        functools.partial(_mix_out_kernel, kw=kw, halo=halo, rb=16, cr=_pick(ts, 64), per_b=per_b),
        grid=(nt + 1,),
        in_specs=[
            pl.BlockSpec((ts, cc), lambda s: (cur(s), 0)),
            pl.BlockSpec((ts, cc), lambda s: (cur(s), 1)),
            const((kw, cc)), const((1, cc)), const((1, cc)), const((1, cc)),
            pl.BlockSpec((None, cc, cc), lambda s: (layer, 0, 0)),
            const((1, cc)),
            pl.BlockSpec((ts, dn), lambda s: (prev(s), 0)),
            pl.BlockSpec((ts, d), lambda s: (prev(s), 0)),
            pl.BlockSpec((None, cc, d), lambda s: (layer, 0, 0)),
            pl.BlockSpec((None, dn, d), lambda s: (layer, cc // dn, 0)),
            pl.BlockSpec((None, 1, d), lambda s: (prev(s) // per_b, 0, 2)),
        ],
        out_specs=pl.BlockSpec((ts, d), lambda s: (prev(s), 0)),
        out_shape=jax.ShapeDtypeStruct((t, d), F32),
        scratch_shapes=[
            pltpu.VMEM((cc // LANES, halo + ts, LANES), F32),
            pltpu.VMEM((ts, cc), F32),
            pltpu.VMEM((ts, cc), BF),
            pltpu.VMEM((2, ts, cc), BF),
        ],
        compiler_params=_params(("arbitrary",)),
        name="mix_out",
    )(proj, proj, dww, row(dwb), row(lnw), row(lnb), pw16, row(onw), y_dn, x2, w_out16, w_out16, mod4)


def _dot_nt(a, b):
    return lax.dot_general(a, b, (((1,), (1,)), ((), ())), preferred_element_type=F32)


def _dot(a, b):
    return jnp.dot(a, b, preferred_element_type=F32)


def _blockdiag(xp, same_chunk):
    g = xp.shape[1] // CHUNK
    return jnp.where(same_chunk, jnp.concatenate([xp] * g, axis=0), 0.0).astype(BF)


def _packed_mm(xs, y, same_chunk):
    a = _dot(jnp.concatenate(list(xs), axis=0).astype(BF), _blockdiag(y, same_chunk))
    return [a[i * CHUNK:(i + 1) * CHUNK] for i in range(len(xs))]


def _packed_unit_lower_inverse(lps, same_chunk):
    ts = lps[0].shape[1]
    pi = lax.broadcasted_iota(jnp.int32, (CHUNK, ts), 0)
    pj = lax.broadcasted_iota(jnp.int32, (CHUNK, ts), 1) % CHUNK
    eye = (pi == pj).astype(F32)
    same16 = (pi // 16) == (pj // 16)
    same32 = (pi // 32) == (pj // 32)
    mm = lambda xs_list, ys: [_packed_mm(xs, y, same_chunk) for xs, y in zip(xs_list, ys)]

    l16 = [jnp.where(same16, lp, 0.0) for lp in lps]
    l2 = [r[0] for r in mm([[a] for a in l16], l16)]
    x = [eye - a for a in l16]
    r = mm([[a, b] for a, b in zip(x, l2)], l2)
    x = [a + b[0] for a, b in zip(x, r)]
    l4 = [b[1] for b in r]
    r = mm([[a, b] for a, b in zip(x, l4)], l4)
    x = [a + b[0] for a, b in zip(x, r)]
    l8 = [b[1] for b in r]
    r = mm([[a] for a in x], l8)
    x = [a + b[0] for a, b in zip(x, r)]
    for keep in (same32 & jnp.logical_not(same16), jnp.logical_not(same32)):
        cm = [jnp.where(keep, lp, 0.0) for lp in lps]
        t1 = [b[0] for b in mm([[a] for a in x], cm)]
        t2 = [b[0] for b in mm([[a] for a in t1], x)]
        x = [a - b for a, b in zip(x, t2)]
    return x


def _delta_kernel(alog_ref, dtb_ref, q_ref, k_ref, v_ref, z_ref, ba_ref, wq_ref, wk_ref, wv_ref,
                  nw_ref, y_ref, state_ref, halo_ref, ext_ref, cols_ref, lp_ref, attn_ref, rhs_ref,
                  qd_ref, kdt_ref, lhs_ref, n_ref, o0_ref, *, heads, hg, sc):
    ts, width = q_ref.shape
    dh = width // hg
    g = ts // CHUNK
    head0 = pl.program_id(1) * hg
    hrows = halo_ref.shape[1]

    @pl.when(pl.program_id(2) == 0)
    def _():
        state_ref[...] = jnp.zeros_like(state_ref)
        halo_ref[...] = jnp.zeros_like(halo_ref)

    pad = [jnp.zeros((SUBLANES - hg, ts), F32)] if hg < SUBLANES else []
    betas, gs = [], []
    for i in range(hg):
        b_logit = ba_ref[pl.ds(head0 + i, 1), :]
        xs = ba_ref[pl.ds(heads + head0 + i, 1), :] + dtb_ref[head0 + i]
        softplus = jnp.maximum(xs, 0.0) + jnp.log(1.0 + jnp.exp(-jnp.abs(xs)))
        betas.append(_sigmoid(b_logit))
        gs.append(-jnp.exp(jnp.full((1, ts), alog_ref[head0 + i], F32)) * softplus)
    beta_rows = jnp.concatenate(betas + pad, axis=0)
    g_rows = jnp.concatenate(gs + pad, axis=0)
    lane = lax.broadcasted_iota(jnp.int32, (SUBLANES, ts), 1) % CHUNK
    d = 1
    while d < CHUNK:
        g_rows = g_rows + jnp.where(lane >= d, pltpu.roll(g_rows, d, 1), 0.0)
        d *= 2
    gl_rows = jnp.where(lane == CHUNK - 1, g_rows, 0.0)
    d = 1
    while d < CHUNK:
        gl_rows = gl_rows + jnp.where(lane + d < CHUNK, pltpu.roll(gl_rows, ts - d, 1), 0.0)
        d *= 2
    kdec_rows = jnp.exp(gl_rows - g_rows)
    rows = jnp.concatenate([beta_rows, g_rows, jnp.zeros((LANES - 2 * SUBLANES, ts), F32)], axis=0)
    cols_ref[...] = rows.T

    ri = lax.broadcasted_iota(jnp.int32, (ts, ts), 0)
    ci = lax.broadcasted_iota(jnp.int32, (ts, ts), 1)
    same_chunk = (ri // CHUNK) == (ci // CHUNK)
    causal = same_chunk & (ri >= ci)

    def short_conv(idx, x_ref, w_ref, lanes):
        ext_ref[pl.ds(0, hrows), :] = halo_ref[idx, :, lanes]
        ext_ref[pl.ds(hrows, ts), :] = x_ref[:, lanes]
        acc = jnp.zeros((ts, dh), F32)
        for j in range(sc):
            acc = acc + ext_ref[pl.ds(hrows - (sc - 1) + j, ts), :] * w_ref[pl.ds(j, 1), lanes]
        halo_ref[idx, :, lanes] = x_ref[pl.ds(ts - hrows, hrows), lanes]
        return _silu(acc)

    for i in range(hg):
        lanes = slice(i * dh, (i + 1) * dh)
        q = short_conv(0, q_ref, wq_ref, lanes)
        k = short_conv(1, k_ref, wk_ref, lanes)
        v = short_conv(2, v_ref, wv_ref, lanes)
        q = q * lax.rsqrt(jnp.sum(q * q, axis=-1, keepdims=True) + EPS) * (dh ** -0.5)
        k = k * lax.rsqrt(jnp.sum(k * k, axis=-1, keepdims=True) + EPS)
        beta_col = cols_ref[:, i:i + 1]
        g_col = cols_ref[:, SUBLANES + i:SUBLANES + i + 1]
        g_row = g_rows[i:i + 1]
        eg = jnp.exp(g_col)
        kb = k * beta_col
        rhs_ref[i] = jnp.concatenate([v * beta_col, kb * eg], axis=1)
        qd_ref[i] = q * eg
        decay = jnp.exp(jnp.where(causal, g_col - g_row, -jnp.inf))
        kq = _dot_nt(jnp.concatenate([kb, q], axis=0).astype(BF), k.astype(BF))
        attn_ref[i] = (kq[ts:] * decay).astype(BF)
        lfull = jnp.where(ri > ci, kq[:ts] * decay, 0.0)
        lp = lfull[0:CHUNK]
        for c in range(1, g):
            lp = lp + lfull[c * CHUNK:(c + 1) * CHUNK]
        lp_ref[i] = lp
        kdt_ref[i] = k.T * kdec_rows[i:i + 1]

    tinv = _packed_unit_lower_inverse([lp_ref[i] for i in range(hg)], same_chunk)

    cj = lax.broadcasted_iota(jnp.int32, (dh, ts), 1) // CHUNK
    eye_p = (lax.broadcasted_iota(jnp.int32, (CHUNK, ts), 0)
             == lax.broadcasted_iota(jnp.int32, (CHUNK, ts), 1) % CHUNK).astype(F32)
    for i in range(hg):
        t_off = _blockdiag(tinv[i] - eye_p, same_chunk)
        rhs = rhs_ref[i]
        uw = rhs + _dot(t_off, rhs.astype(BF))
        uw16 = uw.astype(BF)
        auw = _dot(attn_ref[i], uw16)
        o0_ref[i] = auw[:, :dh]
        qp16 = (qd_ref[i] - auw[:, dh:]).astype(BF)
        kdt = kdt_ref[i]
        for c in range(g):
            kun = _dot(jnp.where(cj == c, kdt, 0.0).astype(BF), uw16)
            n_ref[i, c] = kun[:, :dh]
            lhs_ref[i, c] = jnp.concatenate(
                [kun[:, dh:].astype(BF), qp16[c * CHUNK:(c + 1) * CHUNK]], axis=0)

    nw = nw_ref[...]
    for c in range(g):
        r0 = c * CHUNK
        for i in range(hg):
            lanes = slice(i * dh, (i + 1) * dh)
            state = state_ref[i]
            r = _dot(lhs_ref[i, c], state.astype(BF))
            o = r[dh:] + o0_ref[i, pl.ds(r0, CHUNK), :]
            tail = cols_ref[pl.ds(r0 + CHUNK - SUBLANES, SUBLANES), :]
            g_last = tail[SUBLANES - 1:SUBLANES, SUBLANES + i:SUBLANES + i + 1]
            state_ref[i] = state * jnp.exp(g_last) - r[:dh] + n_ref[i, c]
            on = o * lax.rsqrt(jnp.mean(o * o, axis=-1, keepdims=True) + EPS) * nw
            y_ref[pl.ds(r0, CHUNK), lanes] = (on * _silu(z_ref[pl.ds(r0, CHUNK), lanes])).astype(y_ref.dtype)


def _delta_group(proj, ba_t, conv_w, a_log, dt_bias, nw, batch, seq, cc, heads, dh):
    sc = conv_w.shape[0]
    dn = heads * dh
    hg = _pick(heads, 8)
    width = hg * dh
    assert (2 * cc) % width == 0 and hg <= SUBLANES
    ts = _pick(seq, 256)
    ns = seq // ts
    g = ts // CHUNK
    t = batch * seq
    nba = ba_t.shape[0]
    hrows = _round_up(sc - 1, SUBLANES)
    tok = lambda col0: pl.BlockSpec((ts, width), lambda b, h, s: (b * ns + s, col0 // width + h))
    cw = lambda col0: pl.BlockSpec((sc, width), lambda b, h, s: (0, col0 // width + h))
    smem = pl.BlockSpec(memory_space=pltpu.SMEM)
    return pl.pallas_call(
        functools.partial(_delta_kernel, heads=heads, hg=hg, sc=sc),
        grid=(batch, heads // hg, ns),
        in_specs=[
            smem, smem,
            tok(2 * cc), tok(2 * cc + dn), tok(2 * cc + 2 * dn), tok(2 * cc + 3 * dn),
            pl.BlockSpec((nba, ts), lambda b, h, s: (0, b * ns + s)),
            cw(0), cw(dn), cw(2 * dn),
            pl.BlockSpec((1, dh), lambda b, h, s: (0, 0)),
        ],
        out_specs=pl.BlockSpec((ts, width), lambda b, h, s: (b * ns + s, h)),
        out_shape=jax.ShapeDtypeStruct((t, dn), BF),
        scratch_shapes=[
            pltpu.VMEM((hg, dh, dh), F32),
            pltpu.VMEM((3, hrows, width), F32),
            pltpu.VMEM((hrows + ts, dh), F32),
            pltpu.VMEM((ts, LANES), F32),
            pltpu.VMEM((hg, CHUNK, ts), F32),
            pltpu.VMEM((hg, ts, ts), BF),
            pltpu.VMEM((hg, ts, 2 * dh), F32),
            pltpu.VMEM((hg, ts, dh), F32),
            pltpu.VMEM((hg, dh, ts), F32),
            pltpu.VMEM((hg, g, dh + CHUNK, dh), BF),
            pltpu.VMEM((hg, g, dh, dh), F32),
            pltpu.VMEM((hg, ts, dh), F32),
        ],
        compiler_params=_params(("arbitrary", "arbitrary", "arbitrary")),
        name="delta_group",
    )(a_log, dt_bias, proj, proj, proj, proj, ba_t, conv_w, conv_w, conv_w, nw.reshape(1, dh))


def _mlp_kernel(x_hbm, shift_ref, scale_ref, gate_ref, nw_ref, wu_ref, wd_ref,
                fshift_ref, fscale_ref, fnw_ref, o_ref, h_ref, xbuf_ref, xsem, *, final):
    i = pl.program_id(0)
    j = pl.program_id(1)
    tm = xbuf_ref.shape[0]

    def x_copy(tile):
        return pltpu.make_async_copy(x_hbm.at[pl.ds(tile * tm, tm), :], xbuf_ref, xsem)

    @pl.when(j == 0)
    def _():
        @pl.when(i == 0)
        def _():
            x_copy(0).start()

        x_copy(i).wait()
        _norm_mod_rows(xbuf_ref, nw_ref, shift_ref, scale_ref, h_ref, 16)
        o_ref[...] = xbuf_ref[...]

    @pl.when((j == 1) & (i + 1 < pl.num_programs(0)))
    def _():
        x_copy(i + 1).start()

    a = jnp.maximum(jnp.dot(h_ref[...], wu_ref[...], preferred_element_type=F32), 0.0)
    o_ref[...] += gate_ref[...] * jnp.dot((a * a).astype(BF), wd_ref[...], preferred_element_type=F32)

    if final:
        @pl.when(j == pl.num_programs(1) - 1)
        def _():
            _norm_mod_rows(o_ref, fnw_ref, fshift_ref, fscale_ref, o_ref, 16)


def _mlp(x2, mod4, nw, w_up16, w_down16, layer, modf, fnw, final, seq):
    t, d = x2.shape
    f = w_up16.shape[-1]
    tm = _pick(seq, 1024)
    tf = _pick(f, 1024)
    assert f // tf >= 2
    per_b = seq // tm
    modspec = lambda which: pl.BlockSpec((None, 1, d), lambda i, j: (i // per_b, 0, which))
    return pl.pallas_call(
        functools.partial(_mlp_kernel, final=final),
        grid=(t // tm, f // tf),
        in_specs=[
            pl.BlockSpec(memory_space=pl.ANY),
            modspec(3), modspec(4), modspec(5),
            pl.BlockSpec((1, d), lambda i, j: (0, 0)),
            pl.BlockSpec((None, d, tf), lambda i, j: (layer, 0, j)),
            pl.BlockSpec((None, tf, d), lambda i, j: (layer, j, 0)),
            pl.BlockSpec((None, 1, d), lambda i, j: (i // per_b, 0, 0)),
            pl.BlockSpec((None, 1, d), lambda i, j: (i // per_b, 0, 1)),
            pl.BlockSpec((1, d), lambda i, j: (0, 0)),
        ],
        out_specs=pl.BlockSpec((tm, d), lambda i, j: (i, 0)),
        out_shape=jax.ShapeDtypeStruct((t, d), F32),
        scratch_shapes=[
            pltpu.VMEM((tm, d), BF),
            pltpu.VMEM((tm, d), F32),
            pltpu.SemaphoreType.DMA(()),
        ],
        compiler_params=_params(("arbitrary", "arbitrary")),
        name="mlp",
    )(x2, mod4, mod4, mod4, nw, w_up16, w_down16, modf, modf, fnw)


def kernel(x, c, w_ada, b_ada, norm1_w, w_in, conv_dw_w, conv_dw_b, conv_ln_w, conv_ln_b, w_pw2, conv_out_norm_w, qkv_conv_w, a_log, dt_bias, dn_norm_w, w_out, norm2_w, w_up, w_down, final_ada_w, final_ada_b, final_norm_w):
    batch, seq, d = x.shape
    depth = w_ada.shape[0]
    cc = conv_dw_w.shape[-1]
    heads = a_log.shape[-1]
    dh = dn_norm_w.shape[-1]
    dn = heads * dh
    n_main = 2 * cc + 4 * dn
    assert seq % CHUNK == 0 and dh % LANES == 0 and cc % LANES == 0
    assert w_in.shape[-1] == n_main + 2 * heads

    mods = _ada(c, w_ada, b_ada)
    modf = _ada(c, final_ada_w[None], final_ada_b[None])[0]
    modf = modf.reshape(batch, 1, 2 * d)

    w_in_t16 = _to_bf16(jnp.swapaxes(w_in, 1, 2))
    w_pw16, w_out16, w_up16, w_down16 = _to_bf16(w_pw2), _to_bf16(w_out), _to_bf16(w_up), _to_bf16(w_down)

    x2 = x.reshape(batch * seq, d)
    for l in range(depth):
        mod4 = mods[l].reshape(batch, 1, 6 * d)
        proj, ba_t = _inproj(x2, mod4, norm1_w[l].reshape(1, d), w_in_t16, l, n_main, seq)
        y_dn = _delta_group(proj, ba_t, qkv_conv_w[l], a_log[l], dt_bias[l], dn_norm_w[l],
                            batch, seq, cc, heads, dh)
        x2 = _mix_out(proj, y_dn, x2, conv_dw_w[l], conv_dw_b[l], conv_ln_w[l], conv_ln_b[l],
                      w_pw16, w_out16, l, conv_out_norm_w[l], mod4, seq)
        x2 = _mlp(x2, mod4, norm2_w[l].reshape(1, d), w_up16, w_down16, l,
                  modf, final_norm_w.reshape(1, d), l == depth - 1, seq)
    return x2.reshape(batch, seq, d)
```

```python
import functools

import jax
import jax.numpy as jnp
from jax import lax
from jax.experimental import pallas as pl
from jax.experimental.pallas import tpu as pltpu

EPS = 1e-6
CHUNK = 64
SUBLANES = 8
LANES = 128
VMEM_LIMIT = 56 * 1024 * 1024
CAST_BLOCK_BYTES = 8 * 1024 * 1024
BF = jnp.bfloat16
F32 = jnp.float32


def _round_up(n, m):
    return (n + m - 1) // m * m


def _pick(n, pref):
    t = min(n, pref)
    while n % t:
        t //= 2
    return t


def _params(sem):
    return pltpu.CompilerParams(dimension_semantics=sem, vmem_limit_bytes=VMEM_LIMIT)


def _sigmoid(v):
    return jax.nn.sigmoid(v)


def _silu(v):
    return v * jax.nn.sigmoid(v)


def _cast_kernel(w_ref, o_ref):
    o_ref[...] = w_ref[...].astype(o_ref.dtype)


def _to_bf16(w):
    nl, k, n = w.shape
    pref = max(2 * SUBLANES, CAST_BLOCK_BYTES // (4 * n))
    aligned = [t for t in range(2 * SUBLANES, pref + 1, 2 * SUBLANES) if k % t == 0]
    tk = max(aligned) if aligned else k
    return pl.pallas_call(
        _cast_kernel,
        grid=(nl, k // tk),
        in_specs=[pl.BlockSpec((None, tk, n), lambda l, i: (l, i, 0))],
        out_specs=pl.BlockSpec((None, tk, n), lambda l, i: (l, i, 0)),
        out_shape=jax.ShapeDtypeStruct((nl, k, n), BF),
        compiler_params=_params(("parallel", "parallel")),
        name="to_bf16",
    )(w)


def _ada_kernel(c_ref, w_ref, b_ref, o_ref):
    ca = _silu(c_ref[...]).astype(BF)
    o_ref[...] = jnp.dot(ca, w_ref[...].astype(BF), preferred_element_type=F32) + b_ref[...]


def _ada(c, w, b):
    nl, d, n = w.shape
    nb = c.shape[0]
    tn = _pick(n, 1024)
    return pl.pallas_call(
        _ada_kernel,
        grid=(nl, n // tn),
        in_specs=[
            pl.BlockSpec((nb, d), lambda l, j: (0, 0)),
            pl.BlockSpec((None, d, tn), lambda l, j: (l, 0, j)),
            pl.BlockSpec((None, 1, tn), lambda l, j: (l, 0, j)),
        ],
        out_specs=pl.BlockSpec((None, nb, tn), lambda l, j: (l, 0, j)),
        out_shape=jax.ShapeDtypeStruct((nl, nb, n), F32),
        compiler_params=_params(("parallel", "parallel")),
        name="ada",
    )(c, w, b.reshape(nl, 1, n))


def _norm_mod_rows(x_ref, nw_ref, shift_ref, scale_ref, h_ref, rows_per_step):
    tm = x_ref.shape[0]
    gain = nw_ref[...] * (1.0 + scale_ref[...])
    shift = shift_ref[...]
    for r in range(tm // rows_per_step):
        rows = pl.ds(r * rows_per_step, rows_per_step)
        xr = x_ref[rows, :]
        inv = lax.rsqrt(jnp.mean(xr * xr, axis=-1, keepdims=True) + EPS)
        h_ref[rows, :] = (x_ref[rows, :] * inv * gain + shift).astype(h_ref.dtype)


def _inproj_kernel(x_hbm, shift_ref, scale_ref, nw_ref, w_ref, wba_ref, proj_ref, ba_ref,
                   h_ref, xbuf_ref, xsem):
    i = pl.program_id(0)
    j = pl.program_id(1)
    tm = xbuf_ref.shape[0]

    def x_copy(tile):
        return pltpu.make_async_copy(x_hbm.at[pl.ds(tile * tm, tm), :], xbuf_ref, xsem)

    @pl.when(j == 0)
    def _():
        @pl.when(i == 0)
        def _():
            x_copy(0).start()

        x_copy(i).wait()
        _norm_mod_rows(xbuf_ref, nw_ref, shift_ref, scale_ref, h_ref, 16)
        ba_ref[...] = lax.dot_general(wba_ref[...], h_ref[...], (((1,), (1,)), ((), ())),
                                      preferred_element_type=F32)

    @pl.when((j == 1) & (i + 1 < pl.num_programs(0)))
    def _():
        x_copy(i + 1).start()

    proj_ref[...] = lax.dot_general(h_ref[...], w_ref[...], (((1,), (1,)), ((), ())),
                                    preferred_element_type=F32)


def _inproj(x2, mod4, nw, w_in_t16, layer, n, seq):
    t, d = x2.shape
    nba = w_in_t16.shape[1] - n
    assert n % nba == 0
    tm = _pick(seq, 1024)
    tn = _pick(n, 2048)
    if n // tn < 2:
        tn = _pick(n, tn // 2)
    assert n // tn >= 2
    per_b = seq // tm
    return pl.pallas_call(
        _inproj_kernel,
        grid=(t // tm, n // tn),
        in_specs=[
            pl.BlockSpec(memory_space=pl.ANY),
            pl.BlockSpec((None, 1, d), lambda i, j: (i // per_b, 0, 0)),
            pl.BlockSpec((None, 1, d), lambda i, j: (i // per_b, 0, 1)),
            pl.BlockSpec((1, d), lambda i, j: (0, 0)),
            pl.BlockSpec((None, tn, d), lambda i, j: (layer, j, 0)),
            pl.BlockSpec((None, nba, d), lambda i, j: (layer, n // nba, 0)),
        ],
        out_specs=[
            pl.BlockSpec((tm, tn), lambda i, j: (i, j)),
            pl.BlockSpec((nba, tm), lambda i, j: (0, i)),
        ],
        out_shape=[
            jax.ShapeDtypeStruct((t, n), F32),
            jax.ShapeDtypeStruct((nba, t), F32),
        ],
        scratch_shapes=[
            pltpu.VMEM((tm, d), BF),
            pltpu.VMEM((tm, d), F32),
            pltpu.SemaphoreType.DMA(()),
        ],
        compiler_params=_params(("arbitrary", "arbitrary")),
        name="inproj",
    )(x2, mod4, mod4, nw, w_in_t16, w_in_t16)


def _mix_out_kernel(val_ref, gate_ref, dww_ref, dwb_ref, lnw_ref, lnb_ref, pw_ref, onw_ref,
                    ydn_ref, x_ref, wc_ref, wd_ref, g1_ref, o_ref,
                    ubuf_ref, cbuf_ref, abuf_ref, yc_ref, *, kw, halo, rb, cr, per_b):
    ts, cc = val_ref.shape
    ncol = cc // LANES
    s = pl.program_id(0)

    @pl.when(s == 0)
    def _():
        yc_ref[...] = jnp.zeros_like(yc_ref)

    @pl.when(s % per_b == 0)
    def _():
        ubuf_ref[:, pl.ds(0, halo), :] = jnp.zeros((ncol, halo, LANES), F32)

    for cb in range(ncol):
        lanes = slice(cb * LANES, (cb + 1) * LANES)
        for r0 in range(0, ts, cr):
            rows = pl.ds(r0, cr)
            ubuf_ref[cb, pl.ds(halo + r0, cr), :] = val_ref[rows, lanes] * _sigmoid(gate_ref[rows, lanes])

    first = halo - (kw - 1)
    nblk = o_ref.shape[1] // ncol
    for cb in range(ncol):
        ocols = slice(cb * nblk, (cb + 1) * nblk)
        y = (jnp.dot(yc_ref[(s + 1) % 2], wc_ref[:, ocols], preferred_element_type=F32)
             + jnp.dot(ydn_ref[...], wd_ref[:, ocols], preferred_element_type=F32))
        o_ref[:, ocols] = x_ref[:, ocols] + g1_ref[:, ocols] * y

        lanes = slice(cb * LANES, (cb + 1) * LANES)
        for r0 in range(0, ts, cr):
            acc = jnp.broadcast_to(dwb_ref[:, lanes], (cr, LANES))
            for k in range(kw):
                acc = acc + ubuf_ref[cb, pl.ds(r0 + first + k, cr), :] * dww_ref[pl.ds(k, 1), lanes]
            cbuf_ref[pl.ds(r0, cr), lanes] = acc

    lnw = lnw_ref[...]
    lnb = lnb_ref[...]
    for r in range(ts // rb):
        rows = pl.ds(r * rb, rb)
        acc = cbuf_ref[rows, :]
        mu = jnp.mean(acc, axis=-1, keepdims=True)
        xc = acc - mu
        var = jnp.mean(xc * xc, axis=-1, keepdims=True)
        yn = xc * lax.rsqrt(var + EPS) * lnw + lnb
        abuf_ref[rows, :] = _silu(yn).astype(BF)

    ubuf_ref[:, pl.ds(0, halo), :] = ubuf_ref[:, pl.ds(ts, halo), :]

    p = jnp.dot(abuf_ref[...], pw_ref[...], preferred_element_type=F32)
    inv = lax.rsqrt(jnp.mean(p * p, axis=-1, keepdims=True) + EPS)
    yc_ref[s % 2] = (p * inv * onw_ref[...]).astype(BF)


def _mix_out(proj, y_dn, x2, dww, dwb, lnw, lnb, pw16, w_out16, layer, onw, mod4, seq):
    kw, cc = dww.shape
    t, d = x2.shape
    dn = y_dn.shape[1]
    assert cc % dn == 0
    halo = _round_up(kw - 1, SUBLANES)
    ts = _pick(seq, 512)
    per_b = seq // ts
    nt = t // ts
    row = lambda a: a.reshape(1, cc)
    cur = lambda s: jnp.minimum(s, nt - 1)
    prev = lambda s: jnp.maximum(s - 1, 0)
    const = lambda shape: pl.BlockSpec(shape, lambda s: (0,) * len(shape))
    return pl.pallas_call(
        functools.partial(_mix_out_kernel, kw=kw, halo=halo, rb=16, cr=_pick(ts, 64), per_b=per_b),
        grid=(nt + 1,),
        in_specs=[
            pl.BlockSpec((ts, cc), lambda s: (cur(s), 0)),
            pl.BlockSpec((ts, cc), lambda s: (cur(s), 1)),
            const((kw, cc)), const((1, cc)), const((1, cc)), const((1, cc)),
            pl.BlockSpec((None, cc, cc), lambda s: (layer, 0, 0)),
            const((1, cc)),
            pl.BlockSpec((ts, dn), lambda s: (prev(s), 0)),
            pl.BlockSpec((ts, d), lambda s: (prev(s), 0)),
            pl.BlockSpec((None, cc, d), lambda s: (layer, 0, 0)),
            pl.BlockSpec((None, dn, d), lambda s: (layer, cc // dn, 0)),
            pl.BlockSpec((None, 1, d), lambda s: (prev(s) // per_b, 0, 2)),
        ],
        out_specs=pl.BlockSpec((ts, d), lambda s: (prev(s), 0)),
        out_shape=jax.ShapeDtypeStruct((t, d), F32),
        scratch_shapes=[
            pltpu.VMEM((cc // LANES, halo + ts, LANES), F32),
            pltpu.VMEM((ts, cc), F32),
            pltpu.VMEM((ts, cc), BF),
            pltpu.VMEM((2, ts, cc), BF),
        ],
        compiler_params=_params(("arbitrary",)),
        name="mix_out",
    )(proj, proj, dww, row(dwb), row(lnw), row(lnb), pw16, row(onw), y_dn, x2, w_out16, w_out16, mod4)


def _dot_nt(a, b):
    return lax.dot_general(a, b, (((1,), (1,)), ((), ())), preferred_element_type=F32)


def _dot(a, b):
    return jnp.dot(a, b, preferred_element_type=F32)


def _blockdiag(xp, same_chunk):
    g = xp.shape[1] // CHUNK
    return jnp.where(same_chunk, jnp.concatenate([xp] * g, axis=0), 0.0).astype(BF)


def _packed_mm(xs, y, same_chunk):
    a = _dot(jnp.concatenate(list(xs), axis=0).astype(BF), _blockdiag(y, same_chunk))
    return [a[i * CHUNK:(i + 1) * CHUNK] for i in range(len(xs))]


def _packed_unit_lower_inverse(lps, same_chunk):
    ts = lps[0].shape[1]
    pi = lax.broadcasted_iota(jnp.int32, (CHUNK, ts), 0)
    pj = lax.broadcasted_iota(jnp.int32, (CHUNK, ts), 1) % CHUNK
    eye = (pi == pj).astype(F32)
    same16 = (pi // 16) == (pj // 16)
    same32 = (pi // 32) == (pj // 32)
    mm = lambda xs_list, ys: [_packed_mm(xs, y, same_chunk) for xs, y in zip(xs_list, ys)]

    l16 = [jnp.where(same16, lp, 0.0) for lp in lps]
    l2 = [r[0] for r in mm([[a] for a in l16], l16)]
    x = [eye - a for a in l16]
    r = mm([[a, b] for a, b in zip(x, l2)], l2)
    x = [a + b[0] for a, b in zip(x, r)]
    l4 = [b[1] for b in r]
    r = mm([[a, b] for a, b in zip(x, l4)], l4)
    x = [a + b[0] for a, b in zip(x, r)]
    l8 = [b[1] for b in r]
    r = mm([[a] for a in x], l8)
    x = [a + b[0] for a, b in zip(x, r)]
    for keep in (same32 & jnp.logical_not(same16), jnp.logical_not(same32)):
        cm = [jnp.where(keep, lp, 0.0) for lp in lps]
        t1 = [b[0] for b in mm([[a] for a in x], cm)]
        t2 = [b[0] for b in mm([[a] for a in t1], x)]
        x = [a - b for a, b in zip(x, t2)]
    return x


def _delta_kernel(alog_ref, dtb_ref, q_ref, k_ref, v_ref, z_ref, ba_ref, wq_ref, wk_ref, wv_ref,
                  nw_ref, y_ref, state_ref, halo_ref, ext_ref, cols_ref, lp_ref, attn_ref, rhs_ref,
                  qd_ref, kdt_ref, lhs_ref, n_ref, o0_ref, *, heads, hg, sc):
    ts, width = q_ref.shape
    dh = width // hg
    g = ts // CHUNK
    head0 = pl.program_id(1) * hg
    hrows = halo_ref.shape[1]

    @pl.when(pl.program_id(2) == 0)
    def _():
        state_ref[...] = jnp.zeros_like(state_ref)
        halo_ref[...] = jnp.zeros_like(halo_ref)

    pad = [jnp.zeros((SUBLANES - hg, ts), F32)] if hg < SUBLANES else []
    betas, gs = [], []
    for i in range(hg):
        b_logit = ba_ref[pl.ds(head0 + i, 1), :]
        xs = ba_ref[pl.ds(heads + head0 + i, 1), :] + dtb_ref[head0 + i]
        softplus = jnp.maximum(xs, 0.0) + jnp.log(1.0 + jnp.exp(-jnp.abs(xs)))
        betas.append(_sigmoid(b_logit))
        gs.append(-jnp.exp(jnp.full((1, ts), alog_ref[head0 + i], F32)) * softplus)
    beta_rows = jnp.concatenate(betas + pad, axis=0)
    g_rows = jnp.concatenate(gs + pad, axis=0)
    lane = lax.broadcasted_iota(jnp.int32, (SUBLANES, ts), 1) % CHUNK
    d = 1
    while d < CHUNK:
        g_rows = g_rows + jnp.where(lane >= d, pltpu.roll(g_rows, d, 1), 0.0)
        d *= 2
    gl_rows = jnp.where(lane == CHUNK - 1, g_rows, 0.0)
    d = 1
    while d < CHUNK:
        gl_rows = gl_rows + jnp.where(lane + d < CHUNK, pltpu.roll(gl_rows, ts - d, 1), 0.0)
        d *= 2
    kdec_rows = jnp.exp(gl_rows - g_rows)
    rows = jnp.concatenate([beta_rows, g_rows, jnp.zeros((LANES - 2 * SUBLANES, ts), F32)], axis=0)
    cols_ref[...] = rows.T

    ri = lax.broadcasted_iota(jnp.int32, (ts, ts), 0)
    ci = lax.broadcasted_iota(jnp.int32, (ts, ts), 1)
    same_chunk = (ri // CHUNK) == (ci // CHUNK)
    causal = same_chunk & (ri >= ci)

    def short_conv(idx, x_ref, w_ref, lanes):
        ext_ref[pl.ds(0, hrows), :] = halo_ref[idx, :, lanes]
        ext_ref[pl.ds(hrows, ts), :] = x_ref[:, lanes]
        acc = jnp.zeros((ts, dh), F32)
        for j in range(sc):
            acc = acc + ext_ref[pl.ds(hrows - (sc - 1) + j, ts), :] * w_ref[pl.ds(j, 1), lanes]
        halo_ref[idx, :, lanes] = x_ref[pl.ds(ts - hrows, hrows), lanes]
        return _silu(acc)

    for i in range(hg):
        lanes = slice(i * dh, (i + 1) * dh)
        q = short_conv(0, q_ref, wq_ref, lanes)
        k = short_conv(1, k_ref, wk_ref, lanes)
        v = short_conv(2, v_ref, wv_ref, lanes)
        q = q * lax.rsqrt(jnp.sum(q * q, axis=-1, keepdims=True) + EPS) * (dh ** -0.5)
        k = k * lax.rsqrt(jnp.sum(k * k, axis=-1, keepdims=True) + EPS)
        beta_col = cols_ref[:, i:i + 1]
        g_col = cols_ref[:, SUBLANES + i:SUBLANES + i + 1]
        g_row = g_rows[i:i + 1]
        eg = jnp.exp(g_col)
        kb = k * beta_col
        rhs_ref[i] = jnp.concatenate([v * beta_col, kb * eg], axis=1)
        qd_ref[i] = q * eg
        kb16 = kb.astype(BF)
        q16 = q.astype(BF)
        k16 = k.astype(BF)
        lp = None
        for c in range(g):
            r0 = c * CHUNK
            rows = slice(r0, r0 + CHUNK)
            rs = lax.broadcasted_iota(jnp.int32, (CHUNK, ts), 0) + r0
            cs = lax.broadcasted_iota(jnp.int32, (CHUNK, ts), 1)
            in_chunk = (cs // CHUNK) == c
            g_col_c = cols_ref[pl.ds(r0, CHUNK), SUBLANES + i:SUBLANES + i + 1]
            decay = jnp.exp(jnp.where(in_chunk & (rs >= cs), g_col_c - g_row, -jnp.inf))
            kq = _dot_nt(jnp.concatenate([kb16[rows], q16[rows]], axis=0), k16)
            attn_ref[i, pl.ds(r0, CHUNK), :] = (kq[CHUNK:] * decay).astype(BF)
            lc = jnp.where(rs > cs, kq[:CHUNK] * decay, 0.0)
            lp = lc if lp is None else lp + lc
        lp_ref[i] = lp
        kdt_ref[i] = k.T * kdec_rows[i:i + 1]

    tinv = _packed_unit_lower_inverse([lp_ref[i] for i in range(hg)], same_chunk)

    cj = lax.broadcasted_iota(jnp.int32, (dh, ts), 1) // CHUNK
    eye_p = (lax.broadcasted_iota(jnp.int32, (CHUNK, ts), 0)
             == lax.broadcasted_iota(jnp.int32, (CHUNK, ts), 1) % CHUNK).astype(F32)
    for i in range(hg):
        t_off = _blockdiag(tinv[i] - eye_p, same_chunk)
        rhs = rhs_ref[i]
        uw = rhs + _dot(t_off, rhs.astype(BF))
        uw16 = uw.astype(BF)
        auw = _dot(attn_ref[i], uw16)
        o0_ref[i] = auw[:, :dh]
        qp16 = (qd_ref[i] - auw[:, dh:]).astype(BF)
        kdt = kdt_ref[i]
        for c in range(g):
            kun = _dot(jnp.where(cj == c, kdt, 0.0).astype(BF), uw16)
            n_ref[i, c] = kun[:, :dh]
            lhs_ref[i, c] = jnp.concatenate(
                [kun[:, dh:].astype(BF), qp16[c * CHUNK:(c + 1) * CHUNK]], axis=0)

    nw = nw_ref[...]
    for c in range(g):
        r0 = c * CHUNK
        for i in range(hg):
            lanes = slice(i * dh, (i + 1) * dh)
            state = state_ref[i]
            r = _dot(lhs_ref[i, c], state.astype(BF))
            o = r[dh:] + o0_ref[i, pl.ds(r0, CHUNK), :]
            tail = cols_ref[pl.ds(r0 + CHUNK - SUBLANES, SUBLANES), :]
            g_last = tail[SUBLANES - 1:SUBLANES, SUBLANES + i:SUBLANES + i + 1]
            state_ref[i] = state * jnp.exp(g_last) - r[:dh] + n_ref[i, c]
            on = o * lax.rsqrt(jnp.mean(o * o, axis=-1, keepdims=True) + EPS) * nw
            y_ref[pl.ds(r0, CHUNK), lanes] = (on * _silu(z_ref[pl.ds(r0, CHUNK), lanes])).astype(y_ref.dtype)


def _delta_group(proj, ba_t, conv_w, a_log, dt_bias, nw, batch, seq, cc, heads, dh):
    sc = conv_w.shape[0]
    dn = heads * dh
    hg = _pick(heads, 8)
    width = hg * dh
    assert (2 * cc) % width == 0 and hg <= SUBLANES
    ts = _pick(seq, 256)
    ns = seq // ts
    g = ts // CHUNK
    t = batch * seq
    nba = ba_t.shape[0]
    hrows = _round_up(sc - 1, SUBLANES)
    tok = lambda col0: pl.BlockSpec((ts, width), lambda b, h, s: (b * ns + s, col0 // width + h))
    cw = lambda col0: pl.BlockSpec((sc, width), lambda b, h, s: (0, col0 // width + h))
    smem = pl.BlockSpec(memory_space=pltpu.SMEM)
    return pl.pallas_call(
        functools.partial(_delta_kernel, heads=heads, hg=hg, sc=sc),
        grid=(batch, heads // hg, ns),
        in_specs=[
            smem, smem,
            tok(2 * cc), tok(2 * cc + dn), tok(2 * cc + 2 * dn), tok(2 * cc + 3 * dn),
            pl.BlockSpec((nba, ts), lambda b, h, s: (0, b * ns + s)),
            cw(0), cw(dn), cw(2 * dn),
            pl.BlockSpec((1, dh), lambda b, h, s: (0, 0)),
        ],
        out_specs=pl.BlockSpec((ts, width), lambda b, h, s: (b * ns + s, h)),
        out_shape=jax.ShapeDtypeStruct((t, dn), BF),
        scratch_shapes=[
            pltpu.VMEM((hg, dh, dh), F32),
            pltpu.VMEM((3, hrows, width), F32),
            pltpu.VMEM((hrows + ts, dh), F32),
            pltpu.VMEM((ts, LANES), F32),
            pltpu.VMEM((hg, CHUNK, ts), F32),
            pltpu.VMEM((hg, ts, ts), BF),
            pltpu.VMEM((hg, ts, 2 * dh), F32),
            pltpu.VMEM((hg, ts, dh), F32),
            pltpu.VMEM((hg, dh, ts), F32),
            pltpu.VMEM((hg, g, dh + CHUNK, dh), BF),
            pltpu.VMEM((hg, g, dh, dh), F32),
            pltpu.VMEM((hg, ts, dh), F32),
        ],
        compiler_params=_params(("arbitrary", "arbitrary", "arbitrary")),
        name="delta_group",
    )(a_log, dt_bias, proj, proj, proj, proj, ba_t, conv_w, conv_w, conv_w, nw.reshape(1, dh))


def _mlp_kernel(x_hbm, shift_ref, scale_ref, gate_ref, nw_ref, wu_ref, wd_ref,
                fshift_ref, fscale_ref, fnw_ref, o_ref, h_ref, xbuf_ref, xsem, *, final):
    i = pl.program_id(0)
    j = pl.program_id(1)
    tm = xbuf_ref.shape[0]

    def x_copy(tile):
        return pltpu.make_async_copy(x_hbm.at[pl.ds(tile * tm, tm), :], xbuf_ref, xsem)

    @pl.when(j == 0)
    def _():
        @pl.when(i == 0)
        def _():
            x_copy(0).start()

        x_copy(i).wait()
        _norm_mod_rows(xbuf_ref, nw_ref, shift_ref, scale_ref, h_ref, 16)
        o_ref[...] = xbuf_ref[...]

    @pl.when((j == 1) & (i + 1 < pl.num_programs(0)))
    def _():
        x_copy(i + 1).start()

    a = jnp.maximum(jnp.dot(h_ref[...], wu_ref[...], preferred_element_type=F32), 0.0)
    o_ref[...] += gate_ref[...] * jnp.dot((a * a).astype(BF), wd_ref[...], preferred_element_type=F32)

    if final:
        @pl.when(j == pl.num_programs(1) - 1)
        def _():
            _norm_mod_rows(o_ref, fnw_ref, fshift_ref, fscale_ref, o_ref, 16)


def _mlp(x2, mod4, nw, w_up16, w_down16, layer, modf, fnw, final, seq):
    t, d = x2.shape
    f = w_up16.shape[-1]
    tm = _pick(seq, 1024)
    tf = _pick(f, 1024)
    assert f // tf >= 2
    per_b = seq // tm
    modspec = lambda which: pl.BlockSpec((None, 1, d), lambda i, j: (i // per_b, 0, which))
    return pl.pallas_call(
        functools.partial(_mlp_kernel, final=final),
        grid=(t // tm, f // tf),
        in_specs=[
            pl.BlockSpec(memory_space=pl.ANY),
            modspec(3), modspec(4), modspec(5),
            pl.BlockSpec((1, d), lambda i, j: (0, 0)),
            pl.BlockSpec((None, d, tf), lambda i, j: (layer, 0, j)),
            pl.BlockSpec((None, tf, d), lambda i, j: (layer, j, 0)),
            pl.BlockSpec((None, 1, d), lambda i, j: (i // per_b, 0, 0)),
            pl.BlockSpec((None, 1, d), lambda i, j: (i // per_b, 0, 1)),
            pl.BlockSpec((1, d), lambda i, j: (0, 0)),
        ],
        out_specs=pl.BlockSpec((tm, d), lambda i, j: (i, 0)),
        out_shape=jax.ShapeDtypeStruct((t, d), F32),
        scratch_shapes=[
            pltpu.VMEM((tm, d), BF),
            pltpu.VMEM((tm, d), F32),
            pltpu.SemaphoreType.DMA(()),
        ],
        compiler_params=_params(("arbitrary", "arbitrary")),
        name="mlp",
    )(x2, mod4, mod4, mod4, nw, w_up16, w_down16, modf, modf, fnw)


def kernel(x, c, w_ada, b_ada, norm1_w, w_in, conv_dw_w, conv_dw_b, conv_ln_w, conv_ln_b, w_pw2, conv_out_norm_w, qkv_conv_w, a_log, dt_bias, dn_norm_w, w_out, norm2_w, w_up, w_down, final_ada_w, final_ada_b, final_norm_w):
    batch, seq, d = x.shape
    depth = w_ada.shape[0]
    cc = conv_dw_w.shape[-1]
    heads = a_log.shape[-1]
    dh = dn_norm_w.shape[-1]
    dn = heads * dh
    n_main = 2 * cc + 4 * dn
    assert seq % CHUNK == 0 and dh % LANES == 0 and cc % LANES == 0
    assert w_in.shape[-1] == n_main + 2 * heads

    mods = _ada(c, w_ada, b_ada)
    modf = _ada(c, final_ada_w[None], final_ada_b[None])[0]
    modf = modf.reshape(batch, 1, 2 * d)

    w_in_t16 = _to_bf16(jnp.swapaxes(w_in, 1, 2))
    w_pw16, w_out16, w_up16, w_down16 = _to_bf16(w_pw2), _to_bf16(w_out), _to_bf16(w_up), _to_bf16(w_down)

    x2 = x.reshape(batch * seq, d)
    for l in range(depth):
        mod4 = mods[l].reshape(batch, 1, 6 * d)
        proj, ba_t = _inproj(x2, mod4, norm1_w[l].reshape(1, d), w_in_t16, l, n_main, seq)
        y_dn = _delta_group(proj, ba_t, qkv_conv_w[l], a_log[l], dt_bias[l], dn_norm_w[l],
                            batch, seq, cc, heads, dh)
        x2 = _mix_out(proj, y_dn, x2, conv_dw_w[l], conv_dw_b[l], conv_ln_w[l], conv_ln_b[l],
                      w_pw16, w_out16, l, conv_out_norm_w[l], mod4, seq)
        x2 = _mlp(x2, mod4, norm2_w[l].reshape(1, d), w_up16, w_down16, l,
                  modf, final_norm_w.reshape(1, d), l == depth - 1, seq)
    return x2.reshape(batch, seq, d)
```
